```python
import functools
import jax, jax.numpy as jnp
from jax import lax
import numpy as np

D_MODEL = 4096
BATCH = 4
SEQ = 2048
DEPTH = 1
DEC_BATCH = 128
DEC_SEQ = 8
PAST_LEN = 16384
PAGE_SIZE = 128

MLA_HEADS = 16
QK_NOPE = 128
QK_ROPE = 64
V_HEAD = 128
Q_LORA = 1024
KV_LORA = 512
ROPE_THETA = 10000.0
Q_BLOCK = 128
SM_SCALE = (QK_NOPE + QK_ROPE) ** -0.5
LRU_WIDTH = D_MODEL // 2
LRU_HEADS = 16
LRU_BLOCK = LRU_WIDTH // LRU_HEADS
LRU_CONV = 4
LRU_C = 8.0
D_FF = 11008
FFN_CONV = 3
EPS = 1e-6

ATTN_WIDTH = MLA_HEADS * V_HEAD
MIX_WIDTH = ATTN_WIDTH + LRU_WIDTH
IN_WIDTH = Q_LORA + KV_LORA + QK_ROPE + 2 * LRU_WIDTH
IN_SPLITS = (Q_LORA, Q_LORA + KV_LORA, Q_LORA + KV_LORA + QK_ROPE, Q_LORA + KV_LORA + QK_ROPE + LRU_WIDTH)

kernel_name = 'hymba_mla_rglru_convffn_step'


def rmsnorm(x, g):
    xf = x.astype(jnp.float32)
    y = xf * lax.rsqrt(jnp.mean(xf * xf, axis=-1, keepdims=True) + EPS)
    return (y * g.astype(jnp.float32)).astype(x.dtype)


def rope(x, pos):
    half = QK_ROPE // 2
    freqs = ROPE_THETA ** (-jnp.arange(half, dtype=jnp.float32) * 2.0 / QK_ROPE)
    ang = pos.astype(jnp.float32)[:, None] * freqs
    if x.ndim == 4:
        ang = ang[:, None, :]
    cos, sin = jnp.cos(ang), jnp.sin(ang)
    xf = x.astype(jnp.float32).reshape(x.shape[:-1] + (half, 2))
    x1, x2 = xf[..., 0], xf[..., 1]
    out = jnp.stack([x1 * cos - x2 * sin, x1 * sin + x2 * cos], axis=-1).reshape(x.shape)
    return out.astype(x.dtype)


def causal_conv(x, buf, w, b):
    k, t = w.shape[0], x.shape[1]
    xp = jnp.concatenate([buf.astype(x.dtype), x], axis=1)
    y = b
    for j in range(k):
        y = y + w[j] * xp[:, j:j + t]
    return y.astype(x.dtype), xp[:, t:]


def _lin_combine(e1, e2):
    a1, b1 = e1
    a2, b2 = e2
    return a1 * a2, a2 * b1 + b2


def rglru(x, h0, w_a, b_a, w_x, b_x, lam):
    n, t, _ = x.shape
    f32 = jnp.float32
    xb = x.reshape(n, t, LRU_HEADS, LRU_BLOCK)
    r = jax.nn.sigmoid((jnp.einsum('nthi,hij->nthj', xb, w_a).reshape(n, t, LRU_WIDTH) + b_a).astype(f32))
    i = jax.nn.sigmoid((jnp.einsum('nthi,hij->nthj', xb, w_x).reshape(n, t, LRU_WIDTH) + b_x).astype(f32))
    log_a = -LRU_C * r * jax.nn.softplus(-lam.astype(f32))
    a = jnp.exp(log_a)
    b = jnp.sqrt(-jnp.expm1(2.0 * log_a)) * (i * x.astype(f32))
    b = b.at[:, 0].add(a[:, 0] * h0.astype(f32))
    _, h = lax.associative_scan(_lin_combine, (a, b), axis=1)
    return h.astype(x.dtype), h[:, -1].astype(h0.dtype)


def mla_prompt_attention(q_nope, q_pe, c_kv, k_pe, w_uk, w_uv):
    n, t = q_nope.shape[:2]
    k_nope = jnp.einsum('ntc,chd->nthd', c_kv, w_uk)
    v = jnp.einsum('ntc,chd->nthd', c_kv, w_uv)
    k_pos = jnp.arange(t)

    def block(blk):
        start = blk * Q_BLOCK
        qn = lax.dynamic_slice_in_dim(q_nope, start, Q_BLOCK, axis=1)
        qp = lax.dynamic_slice_in_dim(q_pe, start, Q_BLOCK, axis=1)
        s = (jnp.einsum('nqhd,nkhd->nhqk', qn, k_nope) + jnp.einsum('nqhr,nkr->nhqk', qp, k_pe)).astype(jnp.float32) * SM_SCALE
        q_pos = start + jnp.arange(Q_BLOCK)
        s = jnp.where(k_pos[None, :] <= q_pos[:, None], s, -jnp.inf)
        pr = jax.nn.softmax(s, axis=-1).astype(v.dtype)
        return jnp.einsum('nhqk,nkhd->nqhd', pr, v)

    o = lax.map(block, jnp.arange(t // Q_BLOCK))
    return jnp.moveaxis(o, 0, 1).reshape(n, t, MLA_HEADS, V_HEAD)


def mla_sample_attention(q_nope, q_pe, c_kv, k_pe, w_uk, w_uv, cache_ckv, cache_kpe, page_table, layer):
    f32 = jnp.float32
    t = q_nope.shape[1]
    q_lat = jnp.einsum('nqhd,chd->nqhc', q_nope, w_uk).astype(f32)
    qp = q_pe.astype(f32)

    def scores(ckv, kpe):
        return (jnp.einsum('nqhc,nkc->nqhk', q_lat, ckv) + jnp.einsum('nqhr,nkr->nqhk', qp, kpe)) * SM_SCALE

    ckv_new = c_kv.astype(f32)
    s = scores(ckv_new, k_pe.astype(f32))
    causal = jnp.tril(jnp.ones((t, t), dtype=bool))
    s = jnp.where(causal[None, :, None, :], s, -jnp.inf)
    m = jnp.max(s, axis=-1)
    pr = jnp.exp(s - m[..., None])
    l = jnp.sum(pr, axis=-1)
    acc = jnp.einsum('nqhk,nkc->nqhc', pr, ckv_new)

    def page_step(carry, pages):
        m, l, acc = carry
        ckv = cache_ckv[layer, pages].astype(f32)
        kpe = cache_kpe[layer, pages].astype(f32)
        s = scores(ckv, kpe)
        m_new = jnp.maximum(m, jnp.max(s, axis=-1))
        alpha = jnp.exp(m - m_new)
        pr = jnp.exp(s - m_new[..., None])
        l = alpha * l + jnp.sum(pr, axis=-1)
        acc = alpha[..., None] * acc + jnp.einsum('nqhk,nkc->nqhc', pr, ckv)
        return (m_new, l, acc), None

    (m, l, acc), _ = lax.scan(page_step, (m, l, acc), page_table.T)
    o_lat = acc / l[..., None]
    return jnp.einsum('nqhc,chd->nqhd', o_lat, w_uv.astype(f32)).astype(q_nope.dtype)


def layer_forward(x, pos, attn_fn, lru_h0, lru_buf, ffn_buf, p):
    n, t, _ = x.shape
    h = rmsnorm(x, p['g_mix'])
    z = h @ p['w_in']
    c_q, c_kv, k_pe, xr, yr = jnp.split(z, IN_SPLITS, axis=-1)
    q = jnp.einsum('ntc,chd->nthd', rmsnorm(c_q, p['g_q']), p['w_uq'])
    q_nope, q_pe = q[..., :QK_NOPE], rope(q[..., QK_NOPE:], pos)
    c_kv = rmsnorm(c_kv, p['g_kv'])
    k_pe = rope(k_pe, pos)
    o_attn = attn_fn(q_nope, q_pe, c_kv, k_pe).reshape(n, t, ATTN_WIDTH)
    xc, new_lru_buf = causal_conv(xr, lru_buf, p['lru_conv_w'], p['lru_conv_b'])
    hseq, h_last = rglru(xc, lru_h0, p['lru_w_a'], p['lru_b_a'], p['lru_w_x'], p['lru_b_x'], p['lru_lambda'])
    o_lru = hseq * jax.nn.gelu(yr)
    mix = jnp.concatenate([rmsnorm(o_attn, p['g_attn_out']), rmsnorm(o_lru, p['g_lru_out'])], axis=-1)
    x = x + mix @ p['w_o']
    u = rmsnorm(x, p['g_ffn']) @ p['w_up']
    uc, new_ffn_buf = causal_conv(u, ffn_buf, p['ffn_conv_w'], p['ffn_conv_b'])
    gate, val = jnp.split(uc, 2, axis=-1)
    x = x + (jax.nn.silu(gate) * val) @ p['w_down']
    return x, c_kv, k_pe, h_last, new_lru_buf, new_ffn_buf


def setup_inputs(seed: int = 0) -> dict:
    key = jax.random.key(seed)
    ks = jax.random.split(key, 40)
    f32 = jnp.float32
    n_pages = PAST_LEN // PAGE_SIZE
    n_phys = (5 * DEC_BATCH * n_pages) // 4

    def nrm(k, shape, scale=1.0):
        return jax.random.normal(k, shape, f32) * scale

    def gain(k, shape):
        return 1.0 + 0.05 * jax.random.normal(k, shape, f32)

    page_table = jax.random.permutation(ks[4], n_phys)[:DEC_BATCH * n_pages].reshape(DEC_BATCH, n_pages).astype(jnp.int32)
    a0 = jax.random.uniform(ks[20], (DEPTH, LRU_WIDTH), f32, minval=0.9, maxval=0.999)
    a_base = a0 ** (1.0 / LRU_C)
    lru_lambda = jnp.log(a_base) - jnp.log1p(-a_base)
    return {
        'x_prompt': nrm(ks[0], (BATCH, SEQ, D_MODEL)),
        'x_sample': nrm(ks[1], (DEC_BATCH, DEC_SEQ, D_MODEL)),
        'cache_ckv': nrm(ks[2], (DEPTH, n_phys, PAGE_SIZE, KV_LORA)),
        'cache_kpe': nrm(ks[3], (DEPTH, n_phys, PAGE_SIZE, QK_ROPE)),
        'page_table': page_table,
        'state_lru_h': nrm(ks[5], (DEPTH, DEC_BATCH, LRU_WIDTH), 0.5),
        'state_lru_conv': nrm(ks[6], (DEPTH, DEC_BATCH, LRU_CONV - 1, LRU_WIDTH)),
        'state_ffn_conv': nrm(ks[7], (DEPTH, DEC_BATCH, FFN_CONV - 1, 2 * D_FF)),
        'g_mix': gain(ks[8], (DEPTH, D_MODEL)),
        'w_in': nrm(ks[9], (DEPTH, D_MODEL, IN_WIDTH), D_MODEL ** -0.5),
        'g_q': gain(ks[10], (DEPTH, Q_LORA)),
        'w_uq': nrm(ks[11], (DEPTH, Q_LORA, MLA_HEADS, QK_NOPE + QK_ROPE), Q_LORA ** -0.5),
        'g_kv': gain(ks[12], (DEPTH, KV_LORA)),
        'w_uk': nrm(ks[13], (DEPTH, KV_LORA, MLA_HEADS, QK_NOPE), KV_LORA ** -0.5),
        'w_uv': nrm(ks[14], (DEPTH, KV_LORA, MLA_HEADS, V_HEAD), KV_LORA ** -0.5),
        'lru_conv_w': nrm(ks[15], (DEPTH, LRU_CONV, LRU_WIDTH), LRU_CONV ** -0.5),
        'lru_conv_b': nrm(ks[16], (DEPTH, LRU_WIDTH), 0.01),
        'lru_w_a': nrm(ks[17], (DEPTH, LRU_HEADS, LRU_BLOCK, LRU_BLOCK), LRU_BLOCK ** -0.5),
        'lru_b_a': nrm(ks[18], (DEPTH, LRU_WIDTH), 0.01),
        'lru_w_x': nrm(ks[19], (DEPTH, LRU_HEADS, LRU_BLOCK, LRU_BLOCK), LRU_BLOCK ** -0.5),
        'lru_b_x': nrm(ks[21], (DEPTH, LRU_WIDTH), 0.01),
        'lru_lambda': lru_lambda,
        'g_attn_out': gain(ks[22], (DEPTH, ATTN_WIDTH)),
        'g_lru_out': gain(ks[23], (DEPTH, LRU_WIDTH)),
        'w_o': nrm(ks[24], (DEPTH, MIX_WIDTH, D_MODEL), MIX_WIDTH ** -0.5),
        'g_ffn': gain(ks[25], (DEPTH, D_MODEL)),
        'w_up': nrm(ks[26], (DEPTH, D_MODEL, 2 * D_FF), D_MODEL ** -0.5),
        'ffn_conv_w': nrm(ks[27], (DEPTH, FFN_CONV, 2 * D_FF), FFN_CONV ** -0.5),
        'ffn_conv_b': nrm(ks[28], (DEPTH, 2 * D_FF), 0.01),
        'w_down': nrm(ks[29], (DEPTH, D_FF, D_MODEL), D_FF ** -0.5),
        'g_final': gain(ks[30], (D_MODEL,)),
    }


def reference(x_prompt, x_sample, cache_ckv, cache_kpe, page_table, state_lru_h, state_lru_conv, state_ffn_conv,
              g_mix, w_in, g_q, w_uq, g_kv, w_uk, w_uv, lru_conv_w, lru_conv_b, lru_w_a, lru_b_a, lru_w_x, lru_b_x,
              lru_lambda, g_attn_out, g_lru_out, w_o, g_ffn, w_up, ffn_conv_w, ffn_conv_b, w_down, g_final):
    nb_p, t_p = x_prompt.shape[0], x_prompt.shape[1]
    pos_p = jnp.arange(t_p)
    pos_s = PAST_LEN + jnp.arange(x_sample.shape[1])
    xp, xs = x_prompt, x_sample
    ckv_p, kpe_p, ckv_s, kpe_s = [], [], [], []
    hp, hs, lcp, lcs, fcp, fcs = [], [], [], [], [], []
    for l in range(DEPTH):
        p = dict(g_mix=g_mix[l], w_in=w_in[l], g_q=g_q[l], w_uq=w_uq[l], g_kv=g_kv[l],
                 lru_conv_w=lru_conv_w[l], lru_conv_b=lru_conv_b[l], lru_w_a=lru_w_a[l], lru_b_a=lru_b_a[l],
                 lru_w_x=lru_w_x[l], lru_b_x=lru_b_x[l], lru_lambda=lru_lambda[l], g_attn_out=g_attn_out[l],
                 g_lru_out=g_lru_out[l], w_o=w_o[l], g_ffn=g_ffn[l], w_up=w_up[l], ffn_conv_w=ffn_conv_w[l],
                 ffn_conv_b=ffn_conv_b[l], w_down=w_down[l])
        attn_p = functools.partial(mla_prompt_attention, w_uk=w_uk[l], w_uv=w_uv[l])
        attn_s = functools.partial(mla_sample_attention, w_uk=w_uk[l], w_uv=w_uv[l], cache_ckv=cache_ckv,
                                   cache_kpe=cache_kpe, page_table=page_table, layer=l)
        h0_p = jnp.zeros((nb_p, LRU_WIDTH), xp.dtype)
        lbuf_p = jnp.zeros((nb_p, LRU_CONV - 1, LRU_WIDTH), xp.dtype)
        fbuf_p = jnp.zeros((nb_p, FFN_CONV - 1, 2 * D_FF), xp.dtype)
        xp, c1, k1, h1, lb1, fb1 = layer_forward(xp, pos_p, attn_p, h0_p, lbuf_p, fbuf_p, p)
        xs, c2, k2, h2, lb2, fb2 = layer_forward(xs, pos_s, attn_s, state_lru_h[l], state_lru_conv[l], state_ffn_conv[l], p)
        ckv_p.append(c1); kpe_p.append(k1); ckv_s.append(c2); kpe_s.append(k2)
        hp.append(h1); hs.append(h2); lcp.append(lb1); lcs.append(lb2); fcp.append(fb1); fcs.append(fb2)
    y_prompt = rmsnorm(xp, g_final)
    y_sample = rmsnorm(xs, g_final)
    return (y_prompt, y_sample,
            jnp.stack(ckv_p), jnp.stack(kpe_p), jnp.stack(ckv_s), jnp.stack(kpe_s),
            jnp.stack(hp), jnp.stack(hs), jnp.stack(lcp), jnp.stack(lcs), jnp.stack(fcp), jnp.stack(fcs))
```

```python
import functools

import jax
import jax.numpy as jnp
from jax import lax
from jax.experimental import pallas as pl
from jax.experimental.pallas import tpu as pltpu

F32 = jnp.float32
BF16 = jnp.bfloat16

D_MODEL = 4096
SEQ = 2048
DEC_SEQ = 8
PAST_LEN = 16384
PAGE_SIZE = 128
MLA_HEADS = 16
QK_NOPE = 128
QK_ROPE = 64
V_HEAD = 128
Q_LORA = 1024
KV_LORA = 512
ROPE_THETA = 10000.0
SM_SCALE = (QK_NOPE + QK_ROPE) ** -0.5
LRU_WIDTH = D_MODEL // 2
LRU_HEADS = 16
LRU_BLOCK = LRU_WIDTH // LRU_HEADS
LRU_CONV = 4
LRU_C = 8.0
D_FF = 11008
FFN_CONV = 3
EPS = 1e-6
ATTN_WIDTH = MLA_HEADS * V_HEAD

VMEM_LIMIT_BYTES = 56 * 1024 * 1024
SUBLANES = 8

TM = 512
IN_PAD = 6144
IN_TN = 1024
FF_PAD = 11264
FF_TF = 256
FF_TK = 512
LRU_TC = 256
LRU_SB = 32
ATT_QB = 512
PAGES_PER_STEP = 8


def _cparams(n_axes):
    return pltpu.CompilerParams(dimension_semantics=("arbitrary",) * n_axes,
                                vmem_limit_bytes=VMEM_LIMIT_BYTES)


def _rms(x, g):
    return x * lax.rsqrt(jnp.mean(x * x, axis=-1, keepdims=True) + EPS) * g


def _norm_matmul_kernel(*refs, n_a, has_res):
    a_refs = refs[0:2 * n_a:2]
    g_refs = refs[1:2 * n_a:2]
    w_ref = refs[2 * n_a]
    res_ref = refs[2 * n_a + 1] if has_res else None
    o_ref, h_scr = refs[-2], refs[-1]

    @pl.when(pl.program_id(1) == 0)
    def _():
        off = 0
        for a_ref, g_ref in zip(a_refs, g_refs):
            k = a_ref.shape[-1]
            h_scr[:, off:off + k] = _rms(a_ref[...], g_ref[...]).astype(BF16)
            off += k

    acc = jnp.dot(h_scr[...], w_ref[...], preferred_element_type=F32)
    if has_res:
        acc = acc + res_ref[...]
    o_ref[...] = acc.astype(o_ref.dtype)


def _norm_matmul(a_list, w, res, tn, out_dtype, name):
    m = a_list[0][0].shape[0]
    k_tot, n = w.shape
    in_specs, args = [], []
    for arr, cb, ka, g in a_list:
        in_specs.append(pl.BlockSpec((TM, ka), lambda i, j, cb=cb: (i, cb)))
        in_specs.append(pl.BlockSpec((1, ka), lambda i, j: (0, 0)))
        args += [arr, g]
    in_specs.append(pl.BlockSpec((k_tot, tn), lambda i, j: (0, j)))
    args.append(w)
    if res is not None:
        in_specs.append(pl.BlockSpec((TM, tn), lambda i, j: (i, j)))
        args.append(res)
    return pl.pallas_call(
        functools.partial(_norm_matmul_kernel, n_a=len(a_list), has_res=res is not None),
        grid=(m // TM, n // tn),
        in_specs=in_specs,
        out_specs=pl.BlockSpec((TM, tn), lambda i, j: (i, j)),
        out_shape=jax.ShapeDtypeStruct((m, n), out_dtype),
        scratch_shapes=[pltpu.VMEM((TM, k_tot), BF16)],
        compiler_params=_cparams(2),
        name=name,
    )(*args)


def _q_proj_kernel(a_ref, g_ref, w_ref, cos_ref, sin_ref, qn_ref, qp_ref, h_scr):
    j = pl.program_id(1)

    @pl.when(j == 0)
    def _():
        h_scr[...] = _rms(a_ref[...], g_ref[...]).astype(BF16)

    acc = jnp.dot(h_scr[...], w_ref[...], preferred_element_type=F32)

    @pl.when(j == 0)
    def _():
        for h in range(MLA_HEADS):
            qn_ref[h] = acc[:, h * QK_NOPE:(h + 1) * QK_NOPE].astype(BF16)

    @pl.when(j == 1)
    def _():
        half = MLA_HEADS * QK_ROPE
        roped = acc[:, :half] * cos_ref[...] + acc[:, half:] * sin_ref[...]
        for h in range(MLA_HEADS):
            qp_ref[h] = roped[:, h * QK_ROPE:(h + 1) * QK_ROPE].astype(BF16)


def _q_proj(z, g_q, w_q, cos_q, sin_q, name):
    m = z.shape[0]
    n_tab = cos_q.shape[0] // TM
    half = MLA_HEADS * QK_ROPE
    return pl.pallas_call(
        _q_proj_kernel,
        grid=(m // TM, 2),
        in_specs=[
            pl.BlockSpec((TM, Q_LORA), lambda i, j: (i, 0)),
            pl.BlockSpec((1, Q_LORA), lambda i, j: (0, 0)),
            pl.BlockSpec((Q_LORA, 2 * half), lambda i, j: (0, j)),
            pl.BlockSpec((TM, half), lambda i, j: (i % n_tab, 0)),
            pl.BlockSpec((TM, half), lambda i, j: (i % n_tab, 0)),
        ],
        out_specs=[
            pl.BlockSpec((MLA_HEADS, TM, QK_NOPE), lambda i, j: (0, i, 0)),
            pl.BlockSpec((MLA_HEADS, TM, QK_ROPE), lambda i, j: (0, i, 0)),
        ],
        out_shape=[
            jax.ShapeDtypeStruct((MLA_HEADS, m, QK_NOPE), BF16),
            jax.ShapeDtypeStruct((MLA_HEADS, m, QK_ROPE), BF16),
        ],
        scratch_shapes=[pltpu.VMEM((TM, Q_LORA), BF16)],
        compiler_params=_cparams(2),
        name=name,
    )(z, g_q, w_q, cos_q, sin_q)


def _kv_post_kernel(zkv_ref, zpe_ref, g_ref, cos_ref, sin_ref, ckv_ref, kpe_ref):
    ckv_ref[...] = _rms(zkv_ref[...], g_ref[...])
    zpe = zpe_ref[...]
    kpe_ref[...] = zpe[:, :QK_ROPE] * cos_ref[...] + zpe[:, QK_ROPE:2 * QK_ROPE] * sin_ref[...]


def _kv_post(z, g_kv, cos_k, sin_k, name):
    m = z.shape[0]
    n_tab = cos_k.shape[0] // TM
    return pl.pallas_call(
        _kv_post_kernel,
        grid=(m // TM,),
        in_specs=[
            pl.BlockSpec((TM, KV_LORA), lambda i: (i, Q_LORA // KV_LORA)),
            pl.BlockSpec((TM, KV_LORA), lambda i: (i, Q_LORA // KV_LORA + 1)),
            pl.BlockSpec((1, KV_LORA), lambda i: (0, 0)),
            pl.BlockSpec((TM, QK_ROPE), lambda i: (i % n_tab, 0)),
            pl.BlockSpec((TM, QK_ROPE), lambda i: (i % n_tab, 0)),
        ],
        out_specs=[
            pl.BlockSpec((TM, KV_LORA), lambda i: (i, 0)),
            pl.BlockSpec((TM, QK_ROPE), lambda i: (i, 0)),
        ],
        out_shape=[
            jax.ShapeDtypeStruct((m, KV_LORA), F32),
            jax.ShapeDtypeStruct((m, QK_ROPE), F32),
        ],
        compiler_params=_cparams(1),
        name=name,
    )(z, z, g_kv, cos_k, sin_k)


def _kv_up_kernel(c_ref, w_ref, kn_ref, v_ref):
    acc = jnp.dot(c_ref[...].astype(BF16), w_ref[...], preferred_element_type=F32)
    j = pl.program_id(1)

    @pl.when(j == 0)
    def _():
        for h in range(MLA_HEADS):
            kn_ref[h] = acc[:, h * QK_NOPE:(h + 1) * QK_NOPE].astype(BF16)

    @pl.when(j == 1)
    def _():
        for h in range(MLA_HEADS):
            v_ref[h] = acc[:, h * V_HEAD:(h + 1) * V_HEAD].astype(BF16)


def _kv_up(ckv, w_kv):
    m = ckv.shape[0]
    return pl.pallas_call(
        _kv_up_kernel,
        grid=(m // TM, 2),
        in_specs=[
            pl.BlockSpec((TM, KV_LORA), lambda i, j: (i, 0)),
            pl.BlockSpec((KV_LORA, MLA_HEADS * QK_NOPE), lambda i, j: (0, j)),
        ],
        out_specs=[
            pl.BlockSpec((MLA_HEADS, TM, QK_NOPE), lambda i, j: (0, i, 0)),
            pl.BlockSpec((MLA_HEADS, TM, V_HEAD), lambda i, j: (0, i, 0)),
        ],
        out_shape=[
            jax.ShapeDtypeStruct((MLA_HEADS, m, QK_NOPE), BF16),
            jax.ShapeDtypeStruct((MLA_HEADS, m, V_HEAD), BF16),
        ],
        compiler_params=_cparams(2),
        name="kv_up",
    )(ckv, w_kv)


def _prompt_attn_kernel(qn_ref, qp_ref, kn_ref, kpe_ref, v_ref, o_ref):
    k_cat = jnp.concatenate([kn_ref[...], kpe_ref[...].astype(BF16)], axis=-1)
    for b in range(SEQ // ATT_QB):
        rows = slice(b * ATT_QB, (b + 1) * ATT_QB)
        n_keys = (b + 1) * ATT_QB
        q = jnp.concatenate([qn_ref[rows, :], qp_ref[rows, :]], axis=-1)
        s = lax.dot_general(q, k_cat[:n_keys], (((1,), (1,)), ((), ())),
                            preferred_element_type=F32) * SM_SCALE
        q_pos = b * ATT_QB + lax.broadcasted_iota(jnp.int32, s.shape, 0)
        k_pos = lax.broadcasted_iota(jnp.int32, s.shape, 1)
        s = jnp.where(k_pos <= q_pos, s, -jnp.inf)
        p = jnp.exp(s - jnp.max(s, axis=-1, keepdims=True))
        l = jnp.sum(p, axis=-1, keepdims=True)
        o = jnp.dot(p.astype(BF16), v_ref[:n_keys, :], preferred_element_type=F32)
        o_ref[rows, :] = o / l


def _prompt_attn(qn, qp, kn, kpe, v):
    n_seq = qn.shape[1] // SEQ
    return pl.pallas_call(
        _prompt_attn_kernel,
        grid=(n_seq, MLA_HEADS),
        in_specs=[
            pl.BlockSpec((None, SEQ, QK_NOPE), lambda n, h: (h, n, 0)),
            pl.BlockSpec((None, SEQ, QK_ROPE), lambda n, h: (h, n, 0)),
            pl.BlockSpec((None, SEQ, QK_NOPE), lambda n, h: (h, n, 0)),
            pl.BlockSpec((SEQ, QK_ROPE), lambda n, h: (n, 0)),
            pl.BlockSpec((None, SEQ, V_HEAD), lambda n, h: (h, n, 0)),
        ],
        out_specs=pl.BlockSpec((SEQ, V_HEAD), lambda n, h: (n, h)),
        out_shape=jax.ShapeDtypeStruct((n_seq * SEQ, ATTN_WIDTH), F32),
        compiler_params=_cparams(2),
        name="prompt_attn",
    )(qn, qp, kn, kpe, v)


def _q_lat_kernel(qn_ref, w_ref, o_ref):
    acc = jnp.dot(qn_ref[...], w_ref[...], preferred_element_type=F32)
    o_ref[...] = acc.reshape(o_ref.shape)


def _q_lat(qn_s, w_ukt):
    ms = qn_s.shape[1]
    nb = ms // DEC_SEQ
    return pl.pallas_call(
        _q_lat_kernel,
        grid=(MLA_HEADS,),
        in_specs=[
            pl.BlockSpec((None, ms, QK_NOPE), lambda h: (h, 0, 0)),
            pl.BlockSpec((None, QK_NOPE, KV_LORA), lambda h: (h, 0, 0)),
        ],
        out_specs=pl.BlockSpec((nb, DEC_SEQ, KV_LORA), lambda h: (0, h, 0)),
        out_shape=jax.ShapeDtypeStruct((nb, MLA_HEADS * DEC_SEQ, KV_LORA), F32),
        compiler_params=_cparams(1),
        name="q_lat",
    )(qn_s, w_ukt)


def _paged_attn_kernel(pt_ref, qlat_ref, qpe_ref, cnew_ref, knew_ref, *rest):
    del pt_ref
    g = PAGES_PER_STEP
    ckv_refs, kpe_refs = rest[:g], rest[g:2 * g]
    o_ref, q_scr, m_scr, l_scr, acc_scr = rest[2 * g:]
    pg = pl.program_id(1)
    nt = (((1,), (1,)), ((), ()))

    def online(s, values):
        m_old = m_scr[...]
        m_new = jnp.maximum(m_old, jnp.max(s, axis=-1, keepdims=True))
        alpha = jnp.exp(m_old - m_new)
        p = jnp.exp(s - m_new)
        l_scr[...] = alpha * l_scr[...] + jnp.sum(p, axis=-1, keepdims=True)
        acc_scr[...] = alpha * acc_scr[...] + jnp.dot(p.astype(BF16), values, preferred_element_type=F32)
        m_scr[...] = m_new

    @pl.when(pg == 0)
    def _():
        q_scr[...] = jnp.concatenate([qlat_ref[...].astype(BF16), qpe_ref[...]], axis=-1)
        c_new = cnew_ref[...].astype(BF16)
        k_new = jnp.concatenate([c_new, knew_ref[...].astype(BF16)], axis=-1)
        s = lax.dot_general(q_scr[...], k_new, nt, preferred_element_type=F32) * SM_SCALE
        q_pos = lax.broadcasted_iota(jnp.int32, s.shape, 0) % DEC_SEQ
        k_pos = lax.broadcasted_iota(jnp.int32, s.shape, 1)
        s = jnp.where(k_pos <= q_pos, s, -jnp.inf)
        m_scr[...] = jnp.full(m_scr.shape, -jnp.inf, F32)
        l_scr[...] = jnp.zeros(l_scr.shape, F32)
        acc_scr[...] = jnp.zeros(acc_scr.shape, F32)
        online(s, c_new)

    c_pages = jnp.concatenate([r[...].astype(BF16) for r in ckv_refs], axis=0)
    k_pages = jnp.concatenate([r[...].astype(BF16) for r in kpe_refs], axis=0)
    keys = jnp.concatenate([c_pages, k_pages], axis=-1)
    s = lax.dot_general(q_scr[...], keys, nt, preferred_element_type=F32) * SM_SCALE
    online(s, c_pages)

    @pl.when(pg == pl.num_programs(1) - 1)
    def _():
        o_ref[...] = acc_scr[...] / l_scr[...]


def _paged_attn(page_table, q_lat, q_pe, ckv_new, kpe_new, cache_ckv, cache_kpe):
    nb = q_lat.shape[0]
    rows = MLA_HEADS * DEC_SEQ
    g = PAGES_PER_STEP
    n_pages = page_table.shape[1]

    def page_spec(width, k):
        return pl.BlockSpec((None, None, PAGE_SIZE, width),
                            lambda n, pg, pt, k=k: (0, pt[n, pg * g + k], 0, 0))

    grid_spec = pltpu.PrefetchScalarGridSpec(
        num_scalar_prefetch=1,
        grid=(nb, n_pages // g),
        in_specs=[
            pl.BlockSpec((None, rows, KV_LORA), lambda n, pg, pt: (n, 0, 0)),
            pl.BlockSpec((None, rows, QK_ROPE), lambda n, pg, pt: (n, 0, 0)),
            pl.BlockSpec((DEC_SEQ, KV_LORA), lambda n, pg, pt: (n, 0)),
            pl.BlockSpec((DEC_SEQ, QK_ROPE), lambda n, pg, pt: (n, 0)),
        ] + [page_spec(KV_LORA, k) for k in range(g)] + [page_spec(QK_ROPE, k) for k in range(g)],
        out_specs=pl.BlockSpec((None, rows, KV_LORA), lambda n, pg, pt: (n, 0, 0)),
        scratch_shapes=[
            pltpu.VMEM((rows, KV_LORA + QK_ROPE), BF16),
            pltpu.VMEM((rows, 1), F32),
            pltpu.VMEM((rows, 1), F32),
            pltpu.VMEM((rows, KV_LORA), F32),
        ],
    )
    return pl.pallas_call(
        _paged_attn_kernel,
        grid_spec=grid_spec,
        out_shape=jax.ShapeDtypeStruct((nb, rows, KV_LORA), F32),
        compiler_params=_cparams(2),
        name="paged_attn",
    )(page_table, q_lat, q_pe, ckv_new, kpe_new, *([cache_ckv] * g), *([cache_kpe] * g))


def _o_up_kernel(o_ref, w_ref, out_ref):
    o = o_ref[...]
    o = o.reshape(o.shape[0] * o.shape[1], o.shape[2]).astype(BF16)
    out_ref[...] = jnp.dot(o, w_ref[...], preferred_element_type=F32)


def _o_up(o_lat, w_uvh):
    nb = o_lat.shape[0]
    ms = nb * DEC_SEQ
    return pl.pallas_call(
        _o_up_kernel,
        grid=(MLA_HEADS,),
        in_specs=[
            pl.BlockSpec((nb, DEC_SEQ, KV_LORA), lambda h: (0, h, 0)),
            pl.BlockSpec((None, KV_LORA, V_HEAD), lambda h: (h, 0, 0)),
        ],
        out_specs=pl.BlockSpec((ms, V_HEAD), lambda h: (0, h)),
        out_shape=jax.ShapeDtypeStruct((ms, ATTN_WIDTH), F32),
        compiler_params=_cparams(1),
        name="o_up",
    )(o_lat, w_uvh)


def _lru_gates(xc, wax_ref, ba, bx, sp):
    a_parts, b_parts = [], []
    for blk in range(LRU_HEADS):
        sl = slice(blk * LRU_BLOCK, (blk + 1) * LRU_BLOCK)
        xb = xc[:, sl]
        y = jnp.dot(xb.astype(BF16), wax_ref[blk], preferred_element_type=F32)
        r = jax.nn.sigmoid(y[:, :LRU_BLOCK] + ba[:, sl])
        i_gate = jax.nn.sigmoid(y[:, LRU_BLOCK:] + bx[:, sl])
        a = jnp.exp(-LRU_C * r * sp[:, sl])
        a_parts.append(a)
        b_parts.append(jnp.sqrt(1.0 - a * a) * (i_gate * xb))
    return jnp.concatenate(a_parts, axis=-1), jnp.concatenate(b_parts, axis=-1)


def _lru_prompt_kernel(xr_ref, yr_ref, cw_ref, cb_ref, wax_ref, ba_ref, bx_ref, lam_ref,
                       o_ref, hl_ref, nb_ref, xext, a_scr, b_scr, hs_scr, h_carry):
    c = pl.program_id(1)
    tc = xr_ref.shape[0]
    pad = SUBLANES

    @pl.when(c == 0)
    def _():
        xext[0:pad, :] = jnp.zeros((pad, LRU_WIDTH), F32)
        h_carry[...] = jnp.zeros(h_carry.shape, F32)

    @pl.when(c > 0)
    def _():
        xext[0:pad, :] = xext[tc:tc + pad, :]

    xr = xr_ref[...]
    xext[pad:, :] = xr
    cw = cw_ref[...]
    xc = cb_ref[...]
    for j in range(LRU_CONV - 1):
        xc = xc + cw[j:j + 1, :] * xext[pl.ds(pad - (LRU_CONV - 1) + j, tc), :]
    xc = xc + cw[LRU_CONV - 1:LRU_CONV, :] * xr

    sp = jax.nn.softplus(-lam_ref[...])
    a, b = _lru_gates(xc, wax_ref, ba_ref[...], bx_ref[...], sp)
    a_scr[...] = a
    b_scr[...] = b

    def step(t, h):
        h = a_scr[pl.ds(t, 1), :] * h + b_scr[pl.ds(t, 1), :]
        hs_scr[pl.ds(t, 1), :] = h
        return h

    h = lax.fori_loop(0, tc, step, h_carry[...], unroll=8)
    h_carry[...] = h
    o_ref[...] = hs_scr[...] * jax.nn.gelu(yr_ref[...])
    hl_ref[...] = h
    nb_ref[...] = xext[tc + pad - (LRU_CONV - 1):tc + pad, :]


def _lru_prompt(z, cw, cb, wax, ba, bx, lam):
    m = z.shape[0]
    n_seq = m // SEQ
    n_chunk = SEQ // LRU_TC
    row = lambda: pl.BlockSpec((1, LRU_WIDTH), lambda n, c: (0, 0))
    return pl.pallas_call(
        _lru_prompt_kernel,
        grid=(n_seq, n_chunk),
        in_specs=[
            pl.BlockSpec((LRU_TC, LRU_WIDTH), lambda n, c: (n * n_chunk + c, 1)),
            pl.BlockSpec((LRU_TC, LRU_WIDTH), lambda n, c: (n * n_chunk + c, 2)),
            pl.BlockSpec((LRU_CONV, LRU_WIDTH), lambda n, c: (0, 0)),
            row(),
            pl.BlockSpec((LRU_HEADS, LRU_BLOCK, 2 * LRU_BLOCK), lambda n, c: (0, 0, 0)),
            row(), row(), row(),
        ],
        out_specs=[
            pl.BlockSpec((LRU_TC, LRU_WIDTH), lambda n, c: (n * n_chunk + c, 0)),
            pl.BlockSpec((None, 1, LRU_WIDTH), lambda n, c: (n, 0, 0)),
            pl.BlockSpec((None, LRU_CONV - 1, LRU_WIDTH), lambda n, c: (n, 0, 0)),
        ],
        out_shape=[
            jax.ShapeDtypeStruct((m, LRU_WIDTH), F32),
            jax.ShapeDtypeStruct((n_seq, 1, LRU_WIDTH), F32),
            jax.ShapeDtypeStruct((n_seq, LRU_CONV - 1, LRU_WIDTH), F32),
        ],
        scratch_shapes=[
            pltpu.VMEM((LRU_TC + SUBLANES, LRU_WIDTH), F32),
            pltpu.VMEM((LRU_TC, LRU_WIDTH), F32),
            pltpu.VMEM((LRU_TC, LRU_WIDTH), F32),
            pltpu.VMEM((LRU_TC, LRU_WIDTH), F32),
            pltpu.VMEM((1, LRU_WIDTH), F32),
        ],
        compiler_params=_cparams(2),
        name="lru_prompt",
    )(z, z, cw, cb, wax, ba, bx, lam)


def _lru_sample_kernel(xr_ref, yr_ref, buf_ref, h0_ref, cw_ref, cb_ref, wax_ref, ba_ref, bx_ref, lam_ref,
                       o_ref, hl_ref, nb_ref):
    rows = xr_ref.shape[0]
    sb = rows // DEC_SEQ
    x3 = xr_ref[...].reshape(sb, DEC_SEQ, LRU_WIDTH)
    ext = jnp.concatenate([buf_ref[...], x3], axis=1)
    cw = cw_ref[...]
    xc3 = cb_ref[...][None]
    for j in range(LRU_CONV):
        xc3 = xc3 + cw[j:j + 1, :][None] * ext[:, j:j + DEC_SEQ, :]
    xc = xc3.reshape(rows, LRU_WIDTH)

    sp = jax.nn.softplus(-lam_ref[...])
    a, b = _lru_gates(xc, wax_ref, ba_ref[...], bx_ref[...], sp)
    a3 = a.reshape(sb, DEC_SEQ, LRU_WIDTH)
    b3 = b.reshape(sb, DEC_SEQ, LRU_WIDTH)
    h = h0_ref[...]
    hs = []
    for t in range(DEC_SEQ):
        h = a3[:, t:t + 1, :] * h + b3[:, t:t + 1, :]
        hs.append(h)
    hs3 = jnp.concatenate(hs, axis=1)
    o_ref[...] = hs3.reshape(rows, LRU_WIDTH) * jax.nn.gelu(yr_ref[...])
    hl_ref[...] = h
    nb_ref[...] = ext[:, DEC_SEQ:DEC_SEQ + LRU_CONV - 1, :]


def _lru_sample(z, buf, h0, cw, cb, wax, ba, bx, lam):
    m = z.shape[0]
    nb = m // DEC_SEQ
    rows = LRU_SB * DEC_SEQ
    row = lambda: pl.BlockSpec((1, LRU_WIDTH), lambda s: (0, 0))
    return pl.pallas_call(
        _lru_sample_kernel,
        grid=(nb // LRU_SB,),
        in_specs=[
            pl.BlockSpec((rows, LRU_WIDTH), lambda s: (s, 1)),
            pl.BlockSpec((rows, LRU_WIDTH), lambda s: (s, 2)),
            pl.BlockSpec((LRU_SB, LRU_CONV - 1, LRU_WIDTH), lambda s: (s, 0, 0)),
            pl.BlockSpec((LRU_SB, 1, LRU_WIDTH), lambda s: (s, 0, 0)),
            pl.BlockSpec((LRU_CONV, LRU_WIDTH), lambda s: (0, 0)),
            row(),
            pl.BlockSpec((LRU_HEADS, LRU_BLOCK, 2 * LRU_BLOCK), lambda s: (0, 0, 0)),
            row(), row(), row(),
        ],
        out_specs=[
            pl.BlockSpec((rows, LRU_WIDTH), lambda s: (s, 0)),
            pl.BlockSpec((LRU_SB, 1, LRU_WIDTH), lambda s: (s, 0, 0)),
            pl.BlockSpec((LRU_SB, LRU_CONV - 1, LRU_WIDTH), lambda s: (s, 0, 0)),
        ],
        out_shape=[
            jax.ShapeDtypeStruct((m, LRU_WIDTH), F32),
            jax.ShapeDtypeStruct((nb, 1, LRU_WIDTH), F32),
            jax.ShapeDtypeStruct((nb, LRU_CONV - 1, LRU_WIDTH), F32),
        ],
        compiler_params=_cparams(1),
        name="lru_sample",
    )(z, z, buf, h0, cw, cb, wax, ba, bx, lam)


def _ffn_up_kernel(*refs, sample, tiles_per_seq):
    if sample:
        (x_ref, g_ref, wg_ref, wv_ref, cwg_ref, cwv_ref, cbg_ref, cbv_ref, stg_ref, stv_ref,
         act_ref, bufg_ref, bufv_ref, h_scr) = refs
    else:
        (x_ref, g_ref, wg_ref, wv_ref, cwg_ref, cwv_ref, cbg_ref, cbv_ref,
         act_ref, bufg_ref, bufv_ref, h_scr, halo_g, halo_v, ug_scr, uv_scr) = refs
    i = pl.program_id(0)
    j = pl.program_id(1)
    tm = x_ref.shape[0]
    pad = SUBLANES

    @pl.when(j == 0)
    def _():
        h_scr[...] = _rms(x_ref[...], g_ref[...]).astype(BF16)

    h = h_scr[...]

    def conv_prompt(w_ref, cw_ref, cb_ref, halo, u_scr, buf_ref):
        u = jnp.dot(h, w_ref[...], preferred_element_type=F32)
        first = (i % tiles_per_seq) == 0
        u_scr[0:pad, :] = jnp.where(first, 0.0, halo[j])
        u_scr[pad:, :] = u
        halo[j] = u[tm - pad:, :]
        buf_ref[...] = u[tm - (FFN_CONV - 1):, :]
        cw = cw_ref[...]
        out = cb_ref[...]
        for k in range(FFN_CONV - 1):
            out = out + cw[k:k + 1, :] * u_scr[pl.ds(pad - (FFN_CONV - 1) + k, tm), :]
        return out + cw[FFN_CONV - 1:FFN_CONV, :] * u

    def conv_sample(w_ref, cw_ref, cb_ref, st_ref, buf_ref):
        u = jnp.dot(h, w_ref[...], preferred_element_type=F32)
        tf = u.shape[1]
        u3 = u.reshape(tm // DEC_SEQ, DEC_SEQ, tf)
        ext = jnp.concatenate([st_ref[...], u3], axis=1)
        buf_ref[...] = u3[:, DEC_SEQ - (FFN_CONV - 1):, :]
        cw = cw_ref[...]
        out = cb_ref[...][None]
        for k in range(FFN_CONV):
            out = out + cw[k:k + 1, :][None] * ext[:, k:k + DEC_SEQ, :]
        return out.reshape(tm, tf)

    if sample:
        gate = conv_sample(wg_ref, cwg_ref, cbg_ref, stg_ref, bufg_ref)
        val = conv_sample(wv_ref, cwv_ref, cbv_ref, stv_ref, bufv_ref)
    else:
        gate = conv_prompt(wg_ref, cwg_ref, cbg_ref, halo_g, ug_scr, bufg_ref)
        val = conv_prompt(wv_ref, cwv_ref, cbv_ref, halo_v, uv_scr, bufv_ref)
    act_ref[...] = (jax.nn.silu(gate) * val).astype(BF16)


def _ffn_up(x2, g_ffn, wg, wv, cwg, cwv, cbg, cbv, state=None):
    m = x2.shape[0]
    sample = state is not None
    nj = FF_PAD // FF_TF
    tf = FF_TF
    wspec = lambda: pl.BlockSpec((D_MODEL, tf), lambda i, j: (0, j))
    cwspec = lambda: pl.BlockSpec((FFN_CONV, tf), lambda i, j: (0, j))
    cbspec = lambda: pl.BlockSpec((1, tf), lambda i, j: (0, j))
    in_specs = [
        pl.BlockSpec((TM, D_MODEL), lambda i, j: (i, 0)),
        pl.BlockSpec((1, D_MODEL), lambda i, j: (0, 0)),
        wspec(), wspec(), cwspec(), cwspec(), cbspec(), cbspec(),
    ]
    args = [x2, g_ffn, wg, wv, cwg, cwv, cbg, cbv]
    scratch = [pltpu.VMEM((TM, D_MODEL), BF16)]
    if sample:
        seqs = TM // DEC_SEQ
        n_seq = m // DEC_SEQ
        val0 = D_FF // tf
        last = 2 * D_FF // tf - 1
        in_specs += [
            pl.BlockSpec((seqs, FFN_CONV - 1, tf), lambda i, j: (i, 0, j)),
            pl.BlockSpec((seqs, FFN_CONV - 1, tf), lambda i, j: (i, 0, jnp.minimum(val0 + j, last))),
        ]
        args += [state, state]
        buf_spec = lambda: pl.BlockSpec((seqs, FFN_CONV - 1, tf), lambda i, j: (i, 0, j))
        tiles_per_seq = 1
    else:
        n_seq = m // SEQ
        tiles_per_seq = SEQ // TM
        n_seq = m // TM
        buf_spec = lambda: pl.BlockSpec((None, FFN_CONV - 1, tf), lambda i, j: (i, 0, j))
        scratch += [
            pltpu.VMEM((nj, SUBLANES, tf), F32),
            pltpu.VMEM((nj, SUBLANES, tf), F32),
            pltpu.VMEM((TM + SUBLANES, tf), F32),
            pltpu.VMEM((TM + SUBLANES, tf), F32),
        ]
    buf_shape = jax.ShapeDtypeStruct((n_seq, FFN_CONV - 1, FF_PAD), F32)
    return pl.pallas_call(
        functools.partial(_ffn_up_kernel, sample=sample, tiles_per_seq=tiles_per_seq),
        grid=(m // TM, nj),
        in_specs=in_specs,
        out_specs=[pl.BlockSpec((TM, tf), lambda i, j: (i, j)), buf_spec(), buf_spec()],
        out_shape=[jax.ShapeDtypeStruct((m, FF_PAD), BF16), buf_shape, buf_shape],
        scratch_shapes=scratch,
        compiler_params=_cparams(2),
        name="ffn_up_sample" if sample else "ffn_up_prompt",
    )(*args)


def _ffn_down_kernel(act_ref, w_ref, x_ref, g_ref, o_ref):
    k = pl.program_id(1)

    @pl.when(k == 0)
    def _():
        o_ref[...] = x_ref[...]

    o_ref[...] += jnp.dot(act_ref[...], w_ref[...], preferred_element_type=F32)

    @pl.when(k == pl.num_programs(1) - 1)
    def _():
        o_ref[...] = _rms(o_ref[...], g_ref[...])


def _ffn_down(act, w_down, x2, g_final, name):
    m = act.shape[0]
    return pl.pallas_call(
        _ffn_down_kernel,
        grid=(m // TM, FF_PAD // FF_TK),
        in_specs=[
            pl.BlockSpec((TM, FF_TK), lambda i, k: (i, k)),
            pl.BlockSpec((FF_TK, D_MODEL), lambda i, k: (k, 0)),
            pl.BlockSpec((TM, D_MODEL), lambda i, k: (i, 0)),
            pl.BlockSpec((1, D_MODEL), lambda i, k: (0, 0)),
        ],
        out_specs=pl.BlockSpec((TM, D_MODEL), lambda i, k: (i, 0)),
        out_shape=jax.ShapeDtypeStruct((m, D_MODEL), F32),
        compiler_params=_cparams(2),
        name=name,
    )(act, w_down, x2, g_final)


def _swap_pairs(w):
    shp = w.shape
    return w.reshape(shp[:-1] + (shp[-1] // 2, 2))[..., ::-1].reshape(shp)


def _rope_tables(pos):
    half = QK_ROPE // 2
    freqs = ROPE_THETA ** (-jnp.arange(half, dtype=F32) * 2.0 / QK_ROPE)
    ang = pos.astype(F32)[:, None] * freqs
    cos, sin = jnp.cos(ang), jnp.sin(ang)
    cos_t = jnp.repeat(cos, 2, axis=-1)
    sin_t = jnp.stack([-sin, sin], axis=-1).reshape(pos.shape[0], QK_ROPE)
    return cos_t, sin_t


def _pad_cols(a, n):
    return jnp.pad(a, [(0, 0)] * (a.ndim - 1) + [(0, n - a.shape[-1])])


def kernel(x_prompt, x_sample, cache_ckv, cache_kpe, page_table, state_lru_h, state_lru_conv, state_ffn_conv, g_mix, w_in, g_q, w_uq, g_kv, w_uk, w_uv, lru_conv_w, lru_conv_b, lru_w_a, lru_b_a, lru_w_x, lru_b_x, lru_lambda, g_attn_out, g_lru_out, w_o, g_ffn, w_up, ffn_conv_w, ffn_conv_b, w_down, g_final):
    assert w_in.shape[0] == 1, "single-layer trunk"
    n_p, n_s = x_prompt.shape[0], x_sample.shape[0]
    row = lambda v: v.reshape(1, -1)

    wi = w_in[0]
    c0, c1, c2, c3 = Q_LORA, Q_LORA + KV_LORA, Q_LORA + KV_LORA + QK_ROPE, Q_LORA + KV_LORA + QK_ROPE + LRU_WIDTH
    w_kpe = wi[:, c1:c2]
    zpad = jnp.zeros((D_MODEL, KV_LORA - 2 * QK_ROPE), F32)
    w_in_p = jnp.concatenate([wi[:, :c1], w_kpe, _swap_pairs(w_kpe), zpad, wi[:, c2:c3], wi[:, c3:]],
                             axis=1).astype(BF16)
    wq = w_uq[0]
    wq_rope = wq[:, :, QK_NOPE:]
    w_q_p = jnp.concatenate([wq[:, :, :QK_NOPE].reshape(Q_LORA, -1), wq_rope.reshape(Q_LORA, -1),
                             _swap_pairs(wq_rope).reshape(Q_LORA, -1)], axis=1).astype(BF16)
    w_kv = jnp.concatenate([w_uk[0].reshape(KV_LORA, -1), w_uv[0].reshape(KV_LORA, -1)], axis=1).astype(BF16)
    w_ukt = jnp.transpose(w_uk[0], (1, 2, 0)).astype(BF16)
    w_uvh = jnp.transpose(w_uv[0], (1, 0, 2)).astype(BF16)
    wax = jnp.concatenate([lru_w_a[0], lru_w_x[0]], axis=-1).astype(BF16)
    w_o_b = w_o[0].astype(BF16)
    w_gate = _pad_cols(w_up[0][:, :D_FF], FF_PAD).astype(BF16)
    w_val = _pad_cols(w_up[0][:, D_FF:], FF_PAD).astype(BF16)
    cw_gate = _pad_cols(ffn_conv_w[0][:, :D_FF], FF_PAD)
    cw_val = _pad_cols(ffn_conv_w[0][:, D_FF:], FF_PAD)
    cb_gate = _pad_cols(row(ffn_conv_b[0][:D_FF]), FF_PAD)
    cb_val = _pad_cols(row(ffn_conv_b[0][D_FF:]), FF_PAD)
    w_down_p = jnp.pad(w_down[0], ((0, FF_PAD - D_FF), (0, 0))).astype(BF16)

    cos_p, sin_p = _rope_tables(jnp.arange(SEQ))
    cos_s, sin_s = _rope_tables(PAST_LEN + jnp.arange(DEC_SEQ))
    cos_s, sin_s = jnp.tile(cos_s, (n_s, 1)), jnp.tile(sin_s, (n_s, 1))

    lru_args = (lru_conv_w[0], row(lru_conv_b[0]), wax, row(lru_b_a[0]), row(lru_b_x[0]), row(lru_lambda[0]))

    def trunk(x, cos_k, sin_k, tag):
        z = _norm_matmul([(x, 0, D_MODEL, row(g_mix[0]))], w_in_p, None, IN_TN, F32, "in_proj_" + tag)
        qn, qp = _q_proj(z, row(g_q[0]), w_q_p, jnp.tile(cos_k, (1, MLA_HEADS)), jnp.tile(sin_k, (1, MLA_HEADS)),
                         "q_proj_" + tag)
        ckv, kpe = _kv_post(z, row(g_kv[0]), cos_k, sin_k, "kv_post_" + tag)
        return z, qn, qp, ckv, kpe

    def tail(x, o_attn, o_lru, tag, state=None):
        x2 = _norm_matmul([(o_attn, 0, ATTN_WIDTH, row(g_attn_out[0])), (o_lru, 0, LRU_WIDTH, row(g_lru_out[0]))],
                          w_o_b, x, IN_TN, F32, "out_proj_" + tag)
        act, buf_g, buf_v = _ffn_up(x2, row(g_ffn[0]), w_gate, w_val, cw_gate, cw_val, cb_gate, cb_val, state)
        y = _ffn_down(act, w_down_p, x2, row(g_final), "ffn_down_" + tag)
        if state is None:
            per_seq = SEQ // TM
            buf_g, buf_v = buf_g[per_seq - 1::per_seq], buf_v[per_seq - 1::per_seq]
        new_buf = jnp.concatenate([buf_g[:, :, :D_FF], buf_v[:, :, :D_FF]], axis=-1)
        return y, new_buf

    xp = x_prompt.reshape(n_p * SEQ, D_MODEL)
    z_p, qn_p, qp_p, ckv_p, kpe_p = trunk(xp, cos_p, sin_p, "p")
    kn_p, v_p = _kv_up(ckv_p, w_kv)
    o_attn_p = _prompt_attn(qn_p, qp_p, kn_p, kpe_p, v_p)
    o_lru_p, hl_p, lbuf_p = _lru_prompt(z_p, *lru_args)
    y_p, fbuf_p = tail(xp, o_attn_p, o_lru_p, "p")

    xs = x_sample.reshape(n_s * DEC_SEQ, D_MODEL)
    z_s, qn_s, qp_s, ckv_s, kpe_s = trunk(xs, cos_s, sin_s, "s")
    q_lat = _q_lat(qn_s, w_ukt)
    q_pe = qp_s.reshape(MLA_HEADS, n_s, DEC_SEQ, QK_ROPE).transpose(1, 0, 2, 3).reshape(
        n_s, MLA_HEADS * DEC_SEQ, QK_ROPE)
    o_lat = _paged_attn(page_table, q_lat, q_pe, ckv_s, kpe_s, cache_ckv, cache_kpe)
    o_attn_s = _o_up(o_lat, w_uvh)
    o_lru_s, hl_s, lbuf_s = _lru_sample(z_s, state_lru_conv[0], state_lru_h[0].reshape(n_s, 1, LRU_WIDTH), *lru_args)
    y_s, fbuf_s = tail(xs, o_attn_s, o_lru_s, "s", state_ffn_conv[0])

    return (
        y_p.reshape(n_p, SEQ, D_MODEL),
        y_s.reshape(n_s, DEC_SEQ, D_MODEL),
        ckv_p.reshape(1, n_p, SEQ, KV_LORA),
        kpe_p.reshape(1, n_p, SEQ, QK_ROPE),
        ckv_s.reshape(1, n_s, DEC_SEQ, KV_LORA),
        kpe_s.reshape(1, n_s, DEC_SEQ, QK_ROPE),
        hl_p.reshape(1, n_p, LRU_WIDTH),
        hl_s.reshape(1, n_s, LRU_WIDTH),
        lbuf_p[None],
        lbuf_s[None],
        fbuf_p[None],
        fbuf_s[None],
    )
```

```python
import functools

import jax
import jax.numpy as jnp
from jax import lax
from jax.experimental import pallas as pl
from jax.experimental.pallas import tpu as pltpu

F32 = jnp.float32
BF16 = jnp.bfloat16

D_MODEL = 4096
SEQ = 2048
DEC_SEQ = 8
PAST_LEN = 16384
PAGE_SIZE = 128
MLA_HEADS = 16
QK_NOPE = 128
QK_ROPE = 64
V_HEAD = 128
Q_LORA = 1024
KV_LORA = 512
ROPE_THETA = 10000.0
SM_SCALE = (QK_NOPE + QK_ROPE) ** -0.5
LRU_WIDTH = D_MODEL // 2
LRU_HEADS = 16
LRU_BLOCK = LRU_WIDTH // LRU_HEADS
LRU_CONV = 4
LRU_C = 8.0
D_FF = 11008
FFN_CONV = 3
EPS = 1e-6
ATTN_WIDTH = MLA_HEADS * V_HEAD

VMEM_LIMIT_BYTES = 56 * 1024 * 1024
SUBLANES = 8

TM = 512
IN_PAD = 6144
IN_TN = 1024
FF_PAD = 11264
FF_TM = 1024
FF_TF = 512
FF_CHUNK = 256
FF_TK = 1024
FF_DOWN_CHUNK = 1024
FF_NORM_ROWS = 128
LRU_TC = 256
LRU_SB = 32
ATT_QB = 512
PAGES_PER_STEP = 32
ATT_CHAINS = 4


def _cparams(n_axes):
    return pltpu.CompilerParams(dimension_semantics=("arbitrary",) * n_axes,
                                vmem_limit_bytes=VMEM_LIMIT_BYTES)


def _rms(x, g):
    return x * lax.rsqrt(jnp.mean(x * x, axis=-1, keepdims=True) + EPS) * g


def _norm_matmul_kernel(*refs, n_a, has_res):
    a_refs = refs[0:2 * n_a:2]
    g_refs = refs[1:2 * n_a:2]
    w_ref = refs[2 * n_a]
    res_ref = refs[2 * n_a + 1] if has_res else None
    o_ref, h_scr = refs[-2], refs[-1]

    @pl.when(pl.program_id(1) == 0)
    def _():
        off = 0
        for a_ref, g_ref in zip(a_refs, g_refs):
            k = a_ref.shape[-1]
            h_scr[:, off:off + k] = _rms(a_ref[...], g_ref[...]).astype(BF16)
            off += k

    acc = jnp.dot(h_scr[...], w_ref[...], preferred_element_type=F32)
    if has_res:
        acc = acc + res_ref[...]
    o_ref[...] = acc.astype(o_ref.dtype)


def _norm_matmul(a_list, w, res, tn, out_dtype, name):
    m = a_list[0][0].shape[0]
    k_tot, n = w.shape
    in_specs, args = [], []
    for arr, cb, ka, g in a_list:
        in_specs.append(pl.BlockSpec((TM, ka), lambda i, j, cb=cb: (i, cb)))
        in_specs.append(pl.BlockSpec((1, ka), lambda i, j: (0, 0)))
        args += [arr, g]
    in_specs.append(pl.BlockSpec((k_tot, tn), lambda i, j: (0, j)))
    args.append(w)
    if res is not None:
        in_specs.append(pl.BlockSpec((TM, tn), lambda i, j: (i, j)))
        args.append(res)
    return pl.pallas_call(
        functools.partial(_norm_matmul_kernel, n_a=len(a_list), has_res=res is not None),
        grid=(m // TM, n // tn),
        in_specs=in_specs,
        out_specs=pl.BlockSpec((TM, tn), lambda i, j: (i, j)),
        out_shape=jax.ShapeDtypeStruct((m, n), out_dtype),
        scratch_shapes=[pltpu.VMEM((TM, k_tot), BF16)],
        compiler_params=_cparams(2),
        name=name,
    )(*args)


def _norm_cast_kernel(x_ref, g_ref, o_ref):
    o_ref[...] = _rms(x_ref[...], g_ref[...]).astype(BF16)


def _norm_cast(x, g, name):
    m, k = x.shape
    return pl.pallas_call(
        _norm_cast_kernel,
        grid=(m // TM,),
        in_specs=[pl.BlockSpec((TM, k), lambda i: (i, 0)), pl.BlockSpec((1, k), lambda i: (0, 0))],
        out_specs=pl.BlockSpec((TM, k), lambda i: (i, 0)),
        out_shape=jax.ShapeDtypeStruct((m, k), BF16),
        compiler_params=_cparams(1),
        name=name,
    )(x, g)


def _q_proj_kernel(a_ref, g_ref, w_ref, cos_ref, sin_ref, qn_ref, qp_ref, h_scr):
    j = pl.program_id(1)

    @pl.when(j == 0)
    def _():
        h_scr[...] = _rms(a_ref[...], g_ref[...]).astype(BF16)

    acc = jnp.dot(h_scr[...], w_ref[...], preferred_element_type=F32)

    @pl.when(j == 0)
    def _():
        for h in range(MLA_HEADS):
            qn_ref[h] = acc[:, h * QK_NOPE:(h + 1) * QK_NOPE].astype(BF16)

    @pl.when(j == 1)
    def _():
        half = MLA_HEADS * QK_ROPE
        roped = acc[:, :half] * cos_ref[...] + acc[:, half:] * sin_ref[...]
        for h in range(MLA_HEADS):
            qp_ref[h] = roped[:, h * QK_ROPE:(h + 1) * QK_ROPE].astype(BF16)


def _q_proj(z, g_q, w_q, cos_q, sin_q, name):
    m = z.shape[0]
    n_tab = cos_q.shape[0] // TM
    half = MLA_HEADS * QK_ROPE
    return pl.pallas_call(
        _q_proj_kernel,
        grid=(m // TM, 2),
        in_specs=[
            pl.BlockSpec((TM, Q_LORA), lambda i, j: (i, 0)),
            pl.BlockSpec((1, Q_LORA), lambda i, j: (0, 0)),
            pl.BlockSpec((Q_LORA, 2 * half), lambda i, j: (0, j)),
            pl.BlockSpec((TM, half), lambda i, j: (i % n_tab, 0)),
            pl.BlockSpec((TM, half), lambda i, j: (i % n_tab, 0)),
        ],
        out_specs=[
            pl.BlockSpec((MLA_HEADS, TM, QK_NOPE), lambda i, j: (0, i, 0)),
            pl.BlockSpec((MLA_HEADS, TM, QK_ROPE), lambda i, j: (0, i, 0)),
        ],
        out_shape=[
            jax.ShapeDtypeStruct((MLA_HEADS, m, QK_NOPE), BF16),
            jax.ShapeDtypeStruct((MLA_HEADS, m, QK_ROPE), BF16),
        ],
        scratch_shapes=[pltpu.VMEM((TM, Q_LORA), BF16)],
        compiler_params=_cparams(2),
        name=name,
    )(z, g_q, w_q, cos_q, sin_q)


def _kv_post_kernel(zkv_ref, zpe_ref, g_ref, cos_ref, sin_ref, ckv_ref, kpe_ref):
    ckv_ref[...] = _rms(zkv_ref[...], g_ref[...])
    zpe = zpe_ref[...]
    kpe_ref[...] = zpe[:, :QK_ROPE] * cos_ref[...] + zpe[:, QK_ROPE:2 * QK_ROPE] * sin_ref[...]


def _kv_post(z, g_kv, cos_k, sin_k, name):
    m = z.shape[0]
    n_tab = cos_k.shape[0] // TM
    return pl.pallas_call(
        _kv_post_kernel,
        grid=(m // TM,),
        in_specs=[
            pl.BlockSpec((TM, KV_LORA), lambda i: (i, Q_LORA // KV_LORA)),
            pl.BlockSpec((TM, KV_LORA), lambda i: (i, Q_LORA // KV_LORA + 1)),
            pl.BlockSpec((1, KV_LORA), lambda i: (0, 0)),
            pl.BlockSpec((TM, QK_ROPE), lambda i: (i % n_tab, 0)),
            pl.BlockSpec((TM, QK_ROPE), lambda i: (i % n_tab, 0)),
        ],
        out_specs=[
            pl.BlockSpec((TM, KV_LORA), lambda i: (i, 0)),
            pl.BlockSpec((TM, QK_ROPE), lambda i: (i, 0)),
        ],
        out_shape=[
            jax.ShapeDtypeStruct((m, KV_LORA), F32),
            jax.ShapeDtypeStruct((m, QK_ROPE), F32),
        ],
        compiler_params=_cparams(1),
        name=name,
    )(z, z, g_kv, cos_k, sin_k)


def _kv_up_kernel(c_ref, w_ref, kn_ref, v_ref):
    acc = jnp.dot(c_ref[...].astype(BF16), w_ref[...], preferred_element_type=F32)
    j = pl.program_id(1)

    @pl.when(j == 0)
    def _():
        for h in range(MLA_HEADS):
            kn_ref[h] = acc[:, h * QK_NOPE:(h + 1) * QK_NOPE].astype(BF16)

    @pl.when(j == 1)
    def _():
        for h in range(MLA_HEADS):
            v_ref[h] = acc[:, h * V_HEAD:(h + 1) * V_HEAD].astype(BF16)


def _kv_up(ckv, w_kv):
    m = ckv.shape[0]
    return pl.pallas_call(
        _kv_up_kernel,
        grid=(m // TM, 2),
        in_specs=[
            pl.BlockSpec((TM, KV_LORA), lambda i, j: (i, 0)),
            pl.BlockSpec((KV_LORA, MLA_HEADS * QK_NOPE), lambda i, j: (0, j)),
        ],
        out_specs=[
            pl.BlockSpec((MLA_HEADS, TM, QK_NOPE), lambda i, j: (0, i, 0)),
            pl.BlockSpec((MLA_HEADS, TM, V_HEAD), lambda i, j: (0, i, 0)),
        ],
        out_shape=[
            jax.ShapeDtypeStruct((MLA_HEADS, m, QK_NOPE), BF16),
            jax.ShapeDtypeStruct((MLA_HEADS, m, V_HEAD), BF16),
        ],
        compiler_params=_cparams(2),
        name="kv_up",
    )(ckv, w_kv)


def _prompt_attn_kernel(qn_ref, qp_ref, kn_ref, kpe_ref, v_ref, o_ref):
    k_cat = jnp.concatenate([kn_ref[...], kpe_ref[...].astype(BF16)], axis=-1)
    for b in range(SEQ // ATT_QB):
        rows = slice(b * ATT_QB, (b + 1) * ATT_QB)
        n_keys = (b + 1) * ATT_QB
        q = jnp.concatenate([qn_ref[rows, :], qp_ref[rows, :]], axis=-1)
        s = lax.dot_general(q, k_cat[:n_keys], (((1,), (1,)), ((), ())),
                            preferred_element_type=F32) * SM_SCALE
        q_pos = b * ATT_QB + lax.broadcasted_iota(jnp.int32, s.shape, 0)
        k_pos = lax.broadcasted_iota(jnp.int32, s.shape, 1)
        s = jnp.where(k_pos <= q_pos, s, -jnp.inf)
        p = jnp.exp(s - jnp.max(s, axis=-1, keepdims=True))
        l = jnp.sum(p, axis=-1, keepdims=True)
        o = jnp.dot(p.astype(BF16), v_ref[:n_keys, :], preferred_element_type=F32)
        o_ref[rows, :] = o / l


def _prompt_attn(qn, qp, kn, kpe, v):
    n_seq = qn.shape[1] // SEQ
    return pl.pallas_call(
        _prompt_attn_kernel,
        grid=(n_seq, MLA_HEADS),
        in_specs=[
            pl.BlockSpec((None, SEQ, QK_NOPE), lambda n, h: (h, n, 0)),
            pl.BlockSpec((None, SEQ, QK_ROPE), lambda n, h: (h, n, 0)),
            pl.BlockSpec((None, SEQ, QK_NOPE), lambda n, h: (h, n, 0)),
            pl.BlockSpec((SEQ, QK_ROPE), lambda n, h: (n, 0)),
            pl.BlockSpec((None, SEQ, V_HEAD), lambda n, h: (h, n, 0)),
        ],
        out_specs=pl.BlockSpec((SEQ, V_HEAD), lambda n, h: (n, h)),
        out_shape=jax.ShapeDtypeStruct((n_seq * SEQ, ATTN_WIDTH), F32),
        compiler_params=_cparams(2),
        name="prompt_attn",
    )(qn, qp, kn, kpe, v)


def _q_lat_kernel(qn_ref, w_ref, o_ref):
    acc = jnp.dot(qn_ref[...], w_ref[...], preferred_element_type=F32)
    o_ref[...] = acc.reshape(o_ref.shape)


def _q_lat(qn_s, w_ukt):
    ms = qn_s.shape[1]
    nb = ms // DEC_SEQ
    return pl.pallas_call(
        _q_lat_kernel,
        grid=(MLA_HEADS,),
        in_specs=[
            pl.BlockSpec((None, ms, QK_NOPE), lambda h: (h, 0, 0)),
            pl.BlockSpec((None, QK_NOPE, KV_LORA), lambda h: (h, 0, 0)),
        ],
        out_specs=pl.BlockSpec((nb, DEC_SEQ, KV_LORA), lambda h: (0, h, 0)),
        out_shape=jax.ShapeDtypeStruct((nb, MLA_HEADS * DEC_SEQ, KV_LORA), F32),
        compiler_params=_cparams(1),
        name="q_lat",
    )(qn_s, w_ukt)


def _paged_attn_kernel(pt_ref, qlat_ref, qpe_ref, cnew_ref, knew_ref, *rest):
    del pt_ref
    g = PAGES_PER_STEP
    per = g // ATT_CHAINS
    ckv_refs, kpet_refs = rest[:g], rest[g:2 * g]
    o_ref, ql_scr, kb_scr, m_scr, l_scr, acc_scr = rest[2 * g:]
    pg = pl.program_id(1)
    nt = (((1,), (1,)), ((), ()))

    def online(c, s, values):
        m_old = m_scr[c]
        m_new = jnp.maximum(m_old, jnp.max(s, axis=-1, keepdims=True))
        alpha = jnp.exp(m_old - m_new)
        p = jnp.exp(s - m_new)
        l_scr[c] = alpha * l_scr[c] + jnp.sum(p, axis=-1, keepdims=True)
        acc_scr[c] = alpha * acc_scr[c] + jnp.dot(p.astype(BF16), values, preferred_element_type=F32)
        m_scr[c] = m_new

    @pl.when(pg == 0)
    def _():
        ql_scr[...] = qlat_ref[...].astype(BF16)
        m_scr[...] = jnp.full(m_scr.shape, -jnp.inf, F32)
        l_scr[...] = jnp.zeros(l_scr.shape, F32)
        acc_scr[...] = jnp.zeros(acc_scr.shape, F32)
        c_new = cnew_ref[...].astype(BF16)
        s = (lax.dot_general(ql_scr[...], c_new, nt, preferred_element_type=F32)
             + lax.dot_general(qpe_ref[...], knew_ref[...].astype(BF16), nt, preferred_element_type=F32)) * SM_SCALE
        q_pos = lax.broadcasted_iota(jnp.int32, s.shape, 0) % DEC_SEQ
        k_pos = lax.broadcasted_iota(jnp.int32, s.shape, 1)
        online(0, jnp.where(k_pos <= q_pos, s, -jnp.inf), c_new)

    for c in range(ATT_CHAINS):
        lo = c * per * PAGE_SIZE
        for k in range(per):
            kb_scr[lo + k * PAGE_SIZE:lo + (k + 1) * PAGE_SIZE, :] = ckv_refs[c * per + k][...].astype(BF16)
        kb = kb_scr[lo:lo + per * PAGE_SIZE, :]
        kpt = jnp.concatenate([kpet_refs[c * per + k][...].astype(BF16) for k in range(per)], axis=-1)
        s = (lax.dot_general(ql_scr[...], kb, nt, preferred_element_type=F32)
             + jnp.dot(qpe_ref[...], kpt, preferred_element_type=F32)) * SM_SCALE
        online(c, s, kb)

    @pl.when(pg == pl.num_programs(1) - 1)
    def _():
        m = m_scr[0]
        for c in range(1, ATT_CHAINS):
            m = jnp.maximum(m, m_scr[c])
        l = jnp.zeros(m.shape, F32)
        acc = jnp.zeros(acc_scr.shape[1:], F32)
        for c in range(ATT_CHAINS):
            w = jnp.exp(m_scr[c] - m)
            l = l + w * l_scr[c]
            acc = acc + w * acc_scr[c]
        o_ref[...] = acc / l


def _paged_attn(page_table, q_lat, q_pe, ckv_new, kpe_new, cache_ckv, cache_kpe_t):
    nb = q_lat.shape[0]
    rows = MLA_HEADS * DEC_SEQ
    g = PAGES_PER_STEP
    n_pages = page_table.shape[1]

    def page_spec(shape, k):
        return pl.BlockSpec((None, None) + shape, lambda n, pg, pt, k=k: (0, pt[n, pg * g + k], 0, 0))

    grid_spec = pltpu.PrefetchScalarGridSpec(
        num_scalar_prefetch=1,
        grid=(nb, n_pages // g),
        in_specs=[
            pl.BlockSpec((None, rows, KV_LORA), lambda n, pg, pt: (n, 0, 0)),
            pl.BlockSpec((None, rows, QK_ROPE), lambda n, pg, pt: (n, 0, 0)),
            pl.BlockSpec((DEC_SEQ, KV_LORA), lambda n, pg, pt: (n, 0)),
            pl.BlockSpec((DEC_SEQ, QK_ROPE), lambda n, pg, pt: (n, 0)),
        ] + [page_spec((PAGE_SIZE, KV_LORA), k) for k in range(g)]
          + [page_spec((QK_ROPE, PAGE_SIZE), k) for k in range(g)],
        out_specs=pl.BlockSpec((None, rows, KV_LORA), lambda n, pg, pt: (n, 0, 0)),
        scratch_shapes=[
            pltpu.VMEM((rows, KV_LORA), BF16),
            pltpu.VMEM((g * PAGE_SIZE, KV_LORA), BF16),
            pltpu.VMEM((ATT_CHAINS, rows, 1), F32),
            pltpu.VMEM((ATT_CHAINS, rows, 1), F32),
            pltpu.VMEM((ATT_CHAINS, rows, KV_LORA), F32),
        ],
    )
    return pl.pallas_call(
        _paged_attn_kernel,
        grid_spec=grid_spec,
        out_shape=jax.ShapeDtypeStruct((nb, rows, KV_LORA), F32),
        compiler_params=_cparams(2),
        name="paged_attn",
    )(page_table, q_lat, q_pe, ckv_new, kpe_new, *([cache_ckv] * g), *([cache_kpe_t] * g))


def _o_up_kernel(o_ref, w_ref, out_ref):
    o = o_ref[...]
    o = o.reshape(o.shape[0] * o.shape[1], o.shape[2]).astype(BF16)
    out_ref[...] = jnp.dot(o, w_ref[...], preferred_element_type=F32)


def _o_up(o_lat, w_uvh):
    nb = o_lat.shape[0]
    ms = nb * DEC_SEQ
    return pl.pallas_call(
        _o_up_kernel,
        grid=(MLA_HEADS,),
        in_specs=[
            pl.BlockSpec((nb, DEC_SEQ, KV_LORA), lambda h: (0, h, 0)),
            pl.BlockSpec((None, KV_LORA, V_HEAD), lambda h: (h, 0, 0)),
        ],
        out_specs=pl.BlockSpec((ms, V_HEAD), lambda h: (0, h)),
        out_shape=jax.ShapeDtypeStruct((ms, ATTN_WIDTH), F32),
        compiler_params=_cparams(1),
        name="o_up",
    )(o_lat, w_uvh)


def _lru_gates(xc, wax_ref, ba, bx, sp):
    a_parts, b_parts = [], []
    for blk in range(LRU_HEADS):
        sl = slice(blk * LRU_BLOCK, (blk + 1) * LRU_BLOCK)
        xb = xc[:, sl]
        y = jnp.dot(xb.astype(BF16), wax_ref[blk], preferred_element_type=F32)
        r = jax.nn.sigmoid(y[:, :LRU_BLOCK] + ba[:, sl])
        i_gate = jax.nn.sigmoid(y[:, LRU_BLOCK:] + bx[:, sl])
        a = jnp.exp(-LRU_C * r * sp[:, sl])
        a_parts.append(a)
        b_parts.append(jnp.sqrt(1.0 - a * a) * (i_gate * xb))
    return jnp.concatenate(a_parts, axis=-1), jnp.concatenate(b_parts, axis=-1)


def _lru_prompt_kernel(xr_ref, yr_ref, cw_ref, cb_ref, wax_ref, ba_ref, bx_ref, lam_ref,
                       o_ref, hl_ref, nb_ref, xext, a_scr, b_scr, hs_scr, h_carry):
    c = pl.program_id(1)
    tc = xr_ref.shape[0]
    pad = SUBLANES

    @pl.when(c == 0)
    def _():
        xext[0:pad, :] = jnp.zeros((pad, LRU_WIDTH), F32)
        h_carry[...] = jnp.zeros(h_carry.shape, F32)

    @pl.when(c > 0)
    def _():
        xext[0:pad, :] = xext[tc:tc + pad, :]

    xr = xr_ref[...]
    xext[pad:, :] = xr
    cw = cw_ref[...]
    xc = cb_ref[...]
    for j in range(LRU_CONV - 1):
        xc = xc + cw[j:j + 1, :] * xext[pl.ds(pad - (LRU_CONV - 1) + j, tc), :]
    xc = xc + cw[LRU_CONV - 1:LRU_CONV, :] * xr

    sp = jax.nn.softplus(-lam_ref[...])
    a, b = _lru_gates(xc, wax_ref, ba_ref[...], bx_ref[...], sp)
    a_scr[...] = a
    b_scr[...] = b

    def step(t, h):
        h = a_scr[pl.ds(t, 1), :] * h + b_scr[pl.ds(t, 1), :]
        hs_scr[pl.ds(t, 1), :] = h
        return h

    h = lax.fori_loop(0, tc, step, h_carry[...], unroll=8)
    h_carry[...] = h
    o_ref[...] = hs_scr[...] * jax.nn.gelu(yr_ref[...])
    hl_ref[...] = h
    nb_ref[...] = xext[tc + pad - (LRU_CONV - 1):tc + pad, :]


def _lru_prompt(z, cw, cb, wax, ba, bx, lam):
    m = z.shape[0]
    n_seq = m // SEQ
    n_chunk = SEQ // LRU_TC
    row = lambda: pl.BlockSpec((1, LRU_WIDTH), lambda n, c: (0, 0))
    return pl.pallas_call(
        _lru_prompt_kernel,
        grid=(n_seq, n_chunk),
        in_specs=[
            pl.BlockSpec((LRU_TC, LRU_WIDTH), lambda n, c: (n * n_chunk + c, 1)),
            pl.BlockSpec((LRU_TC, LRU_WIDTH), lambda n, c: (n * n_chunk + c, 2)),
            pl.BlockSpec((LRU_CONV, LRU_WIDTH), lambda n, c: (0, 0)),
            row(),
            pl.BlockSpec((LRU_HEADS, LRU_BLOCK, 2 * LRU_BLOCK), lambda n, c: (0, 0, 0)),
            row(), row(), row(),
        ],
        out_specs=[
            pl.BlockSpec((LRU_TC, LRU_WIDTH), lambda n, c: (n * n_chunk + c, 0)),
            pl.BlockSpec((None, 1, LRU_WIDTH), lambda n, c: (n, 0, 0)),
            pl.BlockSpec((None, LRU_CONV - 1, LRU_WIDTH), lambda n, c: (n, 0, 0)),
        ],
        out_shape=[
            jax.ShapeDtypeStruct((m, LRU_WIDTH), F32),
            jax.ShapeDtypeStruct((n_seq, 1, LRU_WIDTH), F32),
            jax.ShapeDtypeStruct((n_seq, LRU_CONV - 1, LRU_WIDTH), F32),
        ],
        scratch_shapes=[
            pltpu.VMEM((LRU_TC + SUBLANES, LRU_WIDTH), F32),
            pltpu.VMEM((LRU_TC, LRU_WIDTH), F32),
            pltpu.VMEM((LRU_TC, LRU_WIDTH), F32),
            pltpu.VMEM((LRU_TC, LRU_WIDTH), F32),
            pltpu.VMEM((1, LRU_WIDTH), F32),
        ],
        compiler_params=_cparams(2),
        name="lru_prompt",
    )(z, z, cw, cb, wax, ba, bx, lam)


def _lru_sample_kernel(xr_ref, yr_ref, buf_ref, h0_ref, cw_ref, cb_ref, wax_ref, ba_ref, bx_ref, lam_ref,
                       o_ref, hl_ref, nb_ref):
    rows = xr_ref.shape[0]
    sb = rows // DEC_SEQ
    x3 = xr_ref[...].reshape(sb, DEC_SEQ, LRU_WIDTH)
    ext = jnp.concatenate([buf_ref[...], x3], axis=1)
    cw = cw_ref[...]
    xc3 = cb_ref[...][None]
    for j in range(LRU_CONV):
        xc3 = xc3 + cw[j:j + 1, :][None] * ext[:, j:j + DEC_SEQ, :]
    xc = xc3.reshape(rows, LRU_WIDTH)

    sp = jax.nn.softplus(-lam_ref[...])
    a, b = _lru_gates(xc, wax_ref, ba_ref[...], bx_ref[...], sp)
    a3 = a.reshape(sb, DEC_SEQ, LRU_WIDTH)
    b3 = b.reshape(sb, DEC_SEQ, LRU_WIDTH)
    h = h0_ref[...]
    hs = []
    for t in range(DEC_SEQ):
        h = a3[:, t:t + 1, :] * h + b3[:, t:t + 1, :]
        hs.append(h)
    hs3 = jnp.concatenate(hs, axis=1)
    o_ref[...] = hs3.reshape(rows, LRU_WIDTH) * jax.nn.gelu(yr_ref[...])
    hl_ref[...] = h
    nb_ref[...] = ext[:, DEC_SEQ:DEC_SEQ + LRU_CONV - 1, :]


def _lru_sample(z, buf, h0, cw, cb, wax, ba, bx, lam):
    m = z.shape[0]
    nb = m // DEC_SEQ
    rows = LRU_SB * DEC_SEQ
    row = lambda: pl.BlockSpec((1, LRU_WIDTH), lambda s: (0, 0))
    return pl.pallas_call(
        _lru_sample_kernel,
        grid=(nb // LRU_SB,),
        in_specs=[
            pl.BlockSpec((rows, LRU_WIDTH), lambda s: (s, 1)),
            pl.BlockSpec((rows, LRU_WIDTH), lambda s: (s, 2)),
            pl.BlockSpec((LRU_SB, LRU_CONV - 1, LRU_WIDTH), lambda s: (s, 0, 0)),
            pl.BlockSpec((LRU_SB, 1, LRU_WIDTH), lambda s: (s, 0, 0)),
            pl.BlockSpec((LRU_CONV, LRU_WIDTH), lambda s: (0, 0)),
            row(),
            pl.BlockSpec((LRU_HEADS, LRU_BLOCK, 2 * LRU_BLOCK), lambda s: (0, 0, 0)),
            row(), row(), row(),
        ],
        out_specs=[
            pl.BlockSpec((rows, LRU_WIDTH), lambda s: (s, 0)),
            pl.BlockSpec((LRU_SB, 1, LRU_WIDTH), lambda s: (s, 0, 0)),
            pl.BlockSpec((LRU_SB, LRU_CONV - 1, LRU_WIDTH), lambda s: (s, 0, 0)),
        ],
        out_shape=[
            jax.ShapeDtypeStruct((m, LRU_WIDTH), F32),
            jax.ShapeDtypeStruct((nb, 1, LRU_WIDTH), F32),
            jax.ShapeDtypeStruct((nb, LRU_CONV - 1, LRU_WIDTH), F32),
        ],
        compiler_params=_cparams(1),
        name="lru_sample",
    )(z, z, buf, h0, cw, cb, wax, ba, bx, lam)


def _ffn_up_kernel(*refs, sample, tiles_per_seq):
    if sample:
        (h_ref, wg_ref, wv_ref, cwg_ref, cwv_ref, cbg_ref, cbv_ref, stg_ref, stv_ref,
         act_ref, bufg_ref, bufv_ref) = refs
    else:
        (h_ref, wg_ref, wv_ref, cwg_ref, cwv_ref, cbg_ref, cbv_ref,
         act_ref, bufg_ref, bufv_ref, halo_g, halo_v, ug_scr, uv_scr) = refs
    i = pl.program_id(0)
    j = pl.program_id(1)
    tm = h_ref.shape[0]
    pad = SUBLANES
    h = h_ref[...]

    for c in range(FF_TF // FF_CHUNK):
        cols = slice(c * FF_CHUNK, (c + 1) * FF_CHUNK)

        def conv_prompt(w_ref, cw_ref, cb_ref, halo, u_scr, buf_ref):
            u = jnp.dot(h, w_ref[:, cols], preferred_element_type=F32)
            first = (i % tiles_per_seq) == 0
            u_scr[c, 0:pad, :] = jnp.where(first, 0.0, halo[j, :, cols])
            u_scr[c, pad:, :] = u
            halo[j, :, cols] = u[tm - pad:, :]
            buf_ref[:, cols] = u[tm - (FFN_CONV - 1):, :]
            cw = cw_ref[:, cols]
            out = cb_ref[:, cols]
            for k in range(FFN_CONV - 1):
                out = out + cw[k:k + 1, :] * u_scr[c, pl.ds(pad - (FFN_CONV - 1) + k, tm), :]
            return out + cw[FFN_CONV - 1:FFN_CONV, :] * u

        def conv_sample(w_ref, cw_ref, cb_ref, st_ref, buf_ref):
            u = jnp.dot(h, w_ref[:, cols], preferred_element_type=F32)
            u3 = u.reshape(tm // DEC_SEQ, DEC_SEQ, FF_CHUNK)
            ext = jnp.concatenate([st_ref[:, :, cols], u3], axis=1)
            buf_ref[:, :, cols] = u3[:, DEC_SEQ - (FFN_CONV - 1):, :]
            cw = cw_ref[:, cols]
            out = cb_ref[:, cols][None]
            for k in range(FFN_CONV):
                out = out + cw[k:k + 1, :][None] * ext[:, k:k + DEC_SEQ, :]
            return out.reshape(tm, FF_CHUNK)

        if sample:
            gate = conv_sample(wg_ref, cwg_ref, cbg_ref, stg_ref, bufg_ref)
            val = conv_sample(wv_ref, cwv_ref, cbv_ref, stv_ref, bufv_ref)
        else:
            gate = conv_prompt(wg_ref, cwg_ref, cbg_ref, halo_g, ug_scr, bufg_ref)
            val = conv_prompt(wv_ref, cwv_ref, cbv_ref, halo_v, uv_scr, bufv_ref)
        act_ref[:, cols] = (jax.nn.silu(gate) * val).astype(BF16)


def _ffn_up(h2, wg, wv, cwg, cwv, cbg, cbv, state_g=None, state_v=None):
    m = h2.shape[0]
    sample = state_g is not None
    tm, tf = FF_TM, FF_TF
    nj = FF_PAD // tf
    wspec = lambda: pl.BlockSpec((D_MODEL, tf), lambda i, j: (0, j))
    cwspec = lambda: pl.BlockSpec((FFN_CONV, tf), lambda i, j: (0, j))
    cbspec = lambda: pl.BlockSpec((1, tf), lambda i, j: (0, j))
    in_specs = [pl.BlockSpec((tm, D_MODEL), lambda i, j: (i, 0)),
                wspec(), wspec(), cwspec(), cwspec(), cbspec(), cbspec()]
    args = [h2, wg, wv, cwg, cwv, cbg, cbv]
    scratch = []
    if sample:
        seqs = tm // DEC_SEQ
        n_buf = m // DEC_SEQ
        st_spec = lambda: pl.BlockSpec((seqs, FFN_CONV - 1, tf), lambda i, j: (i, 0, j))
        in_specs += [st_spec(), st_spec()]
        args += [state_g, state_v]
        buf_spec = st_spec
        tiles_per_seq = 1
    else:
        tiles_per_seq = SEQ // tm
        n_buf = m // tm
        buf_spec = lambda: pl.BlockSpec((None, FFN_CONV - 1, tf), lambda i, j: (i, 0, j))
        n_chunk = tf // FF_CHUNK
        scratch += [
            pltpu.VMEM((nj, SUBLANES, tf), F32),
            pltpu.VMEM((nj, SUBLANES, tf), F32),
            pltpu.VMEM((n_chunk, tm + SUBLANES, FF_CHUNK), F32),
            pltpu.VMEM((n_chunk, tm + SUBLANES, FF_CHUNK), F32),
        ]
    buf_shape = jax.ShapeDtypeStruct((n_buf, FFN_CONV - 1, FF_PAD), F32)
    return pl.pallas_call(
        functools.partial(_ffn_up_kernel, sample=sample, tiles_per_seq=tiles_per_seq),
        grid=(m // tm, nj),
        in_specs=in_specs,
        out_specs=[pl.BlockSpec((tm, tf), lambda i, j: (i, j)), buf_spec(), buf_spec()],
        out_shape=[jax.ShapeDtypeStruct((m, FF_PAD), BF16), buf_shape, buf_shape],
        scratch_shapes=scratch,
        compiler_params=_cparams(2),
        name="ffn_up_sample" if sample else "ffn_up_prompt",
    )(*args)


def _ffn_down_kernel(act_ref, w_ref, x_ref, g_ref, o_ref):
    k = pl.program_id(1)

    @pl.when(k == 0)
    def _():
        o_ref[...] = x_ref[...]

    act = act_ref[...]
    for c in range(D_MODEL // FF_DOWN_CHUNK):
        cols = slice(c * FF_DOWN_CHUNK, (c + 1) * FF_DOWN_CHUNK)
        o_ref[:, cols] += jnp.dot(act, w_ref[:, cols], preferred_element_type=F32)

    @pl.when(k == pl.num_programs(1) - 1)
    def _():
        for r in range(0, o_ref.shape[0], FF_NORM_ROWS):
            o_ref[r:r + FF_NORM_ROWS, :] = _rms(o_ref[r:r + FF_NORM_ROWS, :], g_ref[...])


def _ffn_down(act, w_down, x2, g_final, name):
    m = act.shape[0]
    return pl.pallas_call(
        _ffn_down_kernel,
        grid=(m // TM, FF_PAD // FF_TK),
        in_specs=[
            pl.BlockSpec((TM, FF_TK), lambda i, k: (i, k)),
            pl.BlockSpec((FF_TK, D_MODEL), lambda i, k: (k, 0)),
            pl.BlockSpec((TM, D_MODEL), lambda i, k: (i, 0)),
            pl.BlockSpec((1, D_MODEL), lambda i, k: (0, 0)),
        ],
        out_specs=pl.BlockSpec((TM, D_MODEL), lambda i, k: (i, 0)),
        out_shape=jax.ShapeDtypeStruct((m, D_MODEL), F32),
        compiler_params=_cparams(2),
        name=name,
    )(act, w_down, x2, g_final)


def _swap_pairs(w):
    shp = w.shape
    return w.reshape(shp[:-1] + (shp[-1] // 2, 2))[..., ::-1].reshape(shp)


def _rope_tables(pos):
    half = QK_ROPE // 2
    freqs = ROPE_THETA ** (-jnp.arange(half, dtype=F32) * 2.0 / QK_ROPE)
    ang = pos.astype(F32)[:, None] * freqs
    cos, sin = jnp.cos(ang), jnp.sin(ang)
    cos_t = jnp.repeat(cos, 2, axis=-1)
    sin_t = jnp.stack([-sin, sin], axis=-1).reshape(pos.shape[0], QK_ROPE)
    return cos_t, sin_t


def _pad_cols(a, n):
    return jnp.pad(a, [(0, 0)] * (a.ndim - 1) + [(0, n - a.shape[-1])])


def kernel(x_prompt, x_sample, cache_ckv, cache_kpe, page_table, state_lru_h, state_lru_conv, state_ffn_conv, g_mix, w_in, g_q, w_uq, g_kv, w_uk, w_uv, lru_conv_w, lru_conv_b, lru_w_a, lru_b_a, lru_w_x, lru_b_x, lru_lambda, g_attn_out, g_lru_out, w_o, g_ffn, w_up, ffn_conv_w, ffn_conv_b, w_down, g_final):
    assert w_in.shape[0] == 1, "single-layer trunk"
    n_p, n_s = x_prompt.shape[0], x_sample.shape[0]
    row = lambda v: v.reshape(1, -1)

    wi = w_in[0]
    c0, c1, c2, c3 = Q_LORA, Q_LORA + KV_LORA, Q_LORA + KV_LORA + QK_ROPE, Q_LORA + KV_LORA + QK_ROPE + LRU_WIDTH
    w_kpe = wi[:, c1:c2]
    zpad = jnp.zeros((D_MODEL, KV_LORA - 2 * QK_ROPE), F32)
    w_in_p = jnp.concatenate([wi[:, :c1], w_kpe, _swap_pairs(w_kpe), zpad, wi[:, c2:c3], wi[:, c3:]],
                             axis=1).astype(BF16)
    wq = w_uq[0]
    wq_rope = wq[:, :, QK_NOPE:]
    w_q_p = jnp.concatenate([wq[:, :, :QK_NOPE].reshape(Q_LORA, -1), wq_rope.reshape(Q_LORA, -1),
                             _swap_pairs(wq_rope).reshape(Q_LORA, -1)], axis=1).astype(BF16)
    w_kv = jnp.concatenate([w_uk[0].reshape(KV_LORA, -1), w_uv[0].reshape(KV_LORA, -1)], axis=1).astype(BF16)
    w_ukt = jnp.transpose(w_uk[0], (1, 2, 0)).astype(BF16)
    w_uvh = jnp.transpose(w_uv[0], (1, 0, 2)).astype(BF16)
    wax = jnp.concatenate([lru_w_a[0], lru_w_x[0]], axis=-1).astype(BF16)
    w_o_b = w_o[0].astype(BF16)
    w_gate = _pad_cols(w_up[0][:, :D_FF], FF_PAD).astype(BF16)
    w_val = _pad_cols(w_up[0][:, D_FF:], FF_PAD).astype(BF16)
    cw_gate = _pad_cols(ffn_conv_w[0][:, :D_FF], FF_PAD)
    cw_val = _pad_cols(ffn_conv_w[0][:, D_FF:], FF_PAD)
    cb_gate = _pad_cols(row(ffn_conv_b[0][:D_FF]), FF_PAD)
    cb_val = _pad_cols(row(ffn_conv_b[0][D_FF:]), FF_PAD)
    w_down_p = jnp.pad(w_down[0], ((0, FF_PAD - D_FF), (0, 0))).astype(BF16)

    cos_p, sin_p = _rope_tables(jnp.arange(SEQ))
    cos_s, sin_s = _rope_tables(PAST_LEN + jnp.arange(DEC_SEQ))
    cos_s, sin_s = jnp.tile(cos_s, (n_s, 1)), jnp.tile(sin_s, (n_s, 1))

    lru_args = (lru_conv_w[0], row(lru_conv_b[0]), wax, row(lru_b_a[0]), row(lru_b_x[0]), row(lru_lambda[0]))

    def trunk(x, cos_k, sin_k, tag):
        z = _norm_matmul([(x, 0, D_MODEL, row(g_mix[0]))], w_in_p, None, IN_TN, F32, "in_proj_" + tag)
        qn, qp = _q_proj(z, row(g_q[0]), w_q_p, jnp.tile(cos_k, (1, MLA_HEADS)), jnp.tile(sin_k, (1, MLA_HEADS)),
                         "q_proj_" + tag)
        ckv, kpe = _kv_post(z, row(g_kv[0]), cos_k, sin_k, "kv_post_" + tag)
        return z, qn, qp, ckv, kpe

    def tail(x, o_attn, o_lru, tag, state=None):
        x2 = _norm_matmul([(o_attn, 0, ATTN_WIDTH, row(g_attn_out[0])), (o_lru, 0, LRU_WIDTH, row(g_lru_out[0]))],
                          w_o_b, x, IN_TN, F32, "out_proj_" + tag)
        h2 = _norm_cast(x2, row(g_ffn[0]), "ffn_norm_" + tag)
        st = (None, None) if state is None else (_pad_cols(state[:, :, :D_FF], FF_PAD),
                                                 _pad_cols(state[:, :, D_FF:], FF_PAD))
        act, buf_g, buf_v = _ffn_up(h2, w_gate, w_val, cw_gate, cw_val, cb_gate, cb_val, *st)
        y = _ffn_down(act, w_down_p, x2, row(g_final), "ffn_down_" + tag)
        if state is None:
            per_seq = SEQ // FF_TM
            buf_g, buf_v = buf_g[per_seq - 1::per_seq], buf_v[per_seq - 1::per_seq]
        new_buf = jnp.concatenate([buf_g[:, :, :D_FF], buf_v[:, :, :D_FF]], axis=-1)
        return y, new_buf

    xp = x_prompt.reshape(n_p * SEQ, D_MODEL)
    z_p, qn_p, qp_p, ckv_p, kpe_p = trunk(xp, cos_p, sin_p, "p")
    kn_p, v_p = _kv_up(ckv_p, w_kv)
    o_attn_p = _prompt_attn(qn_p, qp_p, kn_p, kpe_p, v_p)
    o_lru_p, hl_p, lbuf_p = _lru_prompt(z_p, *lru_args)
    y_p, fbuf_p = tail(xp, o_attn_p, o_lru_p, "p")

    xs = x_sample.reshape(n_s * DEC_SEQ, D_MODEL)
    z_s, qn_s, qp_s, ckv_s, kpe_s = trunk(xs, cos_s, sin_s, "s")
    q_lat = _q_lat(qn_s, w_ukt)
    q_pe = qp_s.reshape(MLA_HEADS, n_s, DEC_SEQ, QK_ROPE).transpose(1, 0, 2, 3).reshape(
        n_s, MLA_HEADS * DEC_SEQ, QK_ROPE)
    o_lat = _paged_attn(page_table, q_lat, q_pe, ckv_s, kpe_s, cache_ckv, jnp.swapaxes(cache_kpe, 2, 3))
    o_attn_s = _o_up(o_lat, w_uvh)
    o_lru_s, hl_s, lbuf_s = _lru_sample(z_s, state_lru_conv[0], state_lru_h[0].reshape(n_s, 1, LRU_WIDTH), *lru_args)
    y_s, fbuf_s = tail(xs, o_attn_s, o_lru_s, "s", state_ffn_conv[0])

    return (
        y_p.reshape(n_p, SEQ, D_MODEL),
        y_s.reshape(n_s, DEC_SEQ, D_MODEL),
        ckv_p.reshape(1, n_p, SEQ, KV_LORA),
        kpe_p.reshape(1, n_p, SEQ, QK_ROPE),
        ckv_s.reshape(1, n_s, DEC_SEQ, KV_LORA),
        kpe_s.reshape(1, n_s, DEC_SEQ, QK_ROPE),
        hl_p.reshape(1, n_p, LRU_WIDTH),
        hl_s.reshape(1, n_s, LRU_WIDTH),
        lbuf_p[None],
        lbuf_s[None],
        fbuf_p[None],
        fbuf_s[None],
    )
```

```python
import functools

import jax
import jax.numpy as jnp
from jax import lax
from jax.experimental import pallas as pl
from jax.experimental.pallas import tpu as pltpu

F32 = jnp.float32
BF16 = jnp.bfloat16

D_MODEL = 4096
SEQ = 2048
DEC_SEQ = 8
PAST_LEN = 16384
PAGE_SIZE = 128
MLA_HEADS = 16
QK_NOPE = 128
QK_ROPE = 64
V_HEAD = 128
Q_LORA = 1024
KV_LORA = 512
ROPE_THETA = 10000.0
SM_SCALE = (QK_NOPE + QK_ROPE) ** -0.5
LRU_WIDTH = D_MODEL // 2
LRU_HEADS = 16
LRU_BLOCK = LRU_WIDTH // LRU_HEADS
LRU_CONV = 4
LRU_C = 8.0
D_FF = 11008
FFN_CONV = 3
EPS = 1e-6
ATTN_WIDTH = MLA_HEADS * V_HEAD

VMEM_LIMIT_BYTES = 56 * 1024 * 1024
SUBLANES = 8

TM = 512
IN_PAD = 6144
IN_TN = 1024
FF_TM = 1024
FF_TF = 256
FF_TK = 1024
FF_DOWN_CHUNK = 1024
FF_NORM_ROWS = 128
LRU_TC = 256
LRU_SB = 32
ATT_QB = 512
PAGE_GROUP = 8
PAGE_SLOTS = 4
ATT_CHAINS = 2


def _cparams(n_axes):
    return pltpu.CompilerParams(dimension_semantics=("arbitrary",) * n_axes,
                                vmem_limit_bytes=VMEM_LIMIT_BYTES)


def _rms(x, g):
    return x * lax.rsqrt(jnp.mean(x * x, axis=-1, keepdims=True) + EPS) * g


def _norm_matmul_kernel(*refs, n_a, has_res):
    a_refs = refs[0:2 * n_a:2]
    g_refs = refs[1:2 * n_a:2]
    w_ref = refs[2 * n_a]
    res_ref = refs[2 * n_a + 1] if has_res else None
    o_ref, h_scr = refs[-2], refs[-1]

    @pl.when(pl.program_id(1) == 0)
    def _():
        off = 0
        for a_ref, g_ref in zip(a_refs, g_refs):
            k = a_ref.shape[-1]
            h_scr[:, off:off + k] = _rms(a_ref[...], g_ref[...]).astype(BF16)
            off += k

    acc = jnp.dot(h_scr[...], w_ref[...], preferred_element_type=F32)
    if has_res:
        acc = acc + res_ref[...]
    o_ref[...] = acc.astype(o_ref.dtype)


def _norm_matmul(a_list, w, res, tn, out_dtype, name):
    m = a_list[0][0].shape[0]
    k_tot, n = w.shape
    in_specs, args = [], []
    for arr, cb, ka, g in a_list:
        in_specs.append(pl.BlockSpec((TM, ka), lambda i, j, cb=cb: (i, cb)))
        in_specs.append(pl.BlockSpec((1, ka), lambda i, j: (0, 0)))
        args += [arr, g]
    in_specs.append(pl.BlockSpec((k_tot, tn), lambda i, j: (0, j)))
    args.append(w)
    if res is not None:
        in_specs.append(pl.BlockSpec((TM, tn), lambda i, j: (i, j)))
        args.append(res)
    return pl.pallas_call(
        functools.partial(_norm_matmul_kernel, n_a=len(a_list), has_res=res is not None),
        grid=(m // TM, n // tn),
        in_specs=in_specs,
        out_specs=pl.BlockSpec((TM, tn), lambda i, j: (i, j)),
        out_shape=jax.ShapeDtypeStruct((m, n), out_dtype),
        scratch_shapes=[pltpu.VMEM((TM, k_tot), BF16)],
        compiler_params=_cparams(2),
        name=name,
    )(*args)


def _norm_cast_kernel(x_ref, g_ref, o_ref):
    o_ref[...] = _rms(x_ref[...], g_ref[...]).astype(BF16)


def _norm_cast(x, g, name):
    m, k = x.shape
    return pl.pallas_call(
        _norm_cast_kernel,
        grid=(m // TM,),
        in_specs=[pl.BlockSpec((TM, k), lambda i: (i, 0)), pl.BlockSpec((1, k), lambda i: (0, 0))],
        out_specs=pl.BlockSpec((TM, k), lambda i: (i, 0)),
        out_shape=jax.ShapeDtypeStruct((m, k), BF16),
        compiler_params=_cparams(1),
        name=name,
    )(x, g)


def _q_proj_kernel(a_ref, g_ref, w_ref, cos_ref, sin_ref, qn_ref, qp_ref, h_scr):
    j = pl.program_id(1)

    @pl.when(j == 0)
    def _():
        h_scr[...] = _rms(a_ref[...], g_ref[...]).astype(BF16)

    acc = jnp.dot(h_scr[...], w_ref[...], preferred_element_type=F32)

    @pl.when(j == 0)
    def _():
        for h in range(MLA_HEADS):
            qn_ref[h] = acc[:, h * QK_NOPE:(h + 1) * QK_NOPE].astype(BF16)

    @pl.when(j == 1)
    def _():
        half = MLA_HEADS * QK_ROPE
        roped = acc[:, :half] * cos_ref[...] + acc[:, half:] * sin_ref[...]
        for h in range(MLA_HEADS):
            qp_ref[h] = roped[:, h * QK_ROPE:(h + 1) * QK_ROPE].astype(BF16)


def _q_proj(z, g_q, w_q, cos_q, sin_q, name):
    m = z.shape[0]
    n_tab = cos_q.shape[0] // TM
    half = MLA_HEADS * QK_ROPE
    return pl.pallas_call(
        _q_proj_kernel,
        grid=(m // TM, 2),
        in_specs=[
            pl.BlockSpec((TM, Q_LORA), lambda i, j: (i, 0)),
            pl.BlockSpec((1, Q_LORA), lambda i, j: (0, 0)),
            pl.BlockSpec((Q_LORA, 2 * half), lambda i, j: (0, j)),
            pl.BlockSpec((TM, half), lambda i, j: (i % n_tab, 0)),
            pl.BlockSpec((TM, half), lambda i, j: (i % n_tab, 0)),
        ],
        out_specs=[
            pl.BlockSpec((MLA_HEADS, TM, QK_NOPE), lambda i, j: (0, i, 0)),
            pl.BlockSpec((MLA_HEADS, TM, QK_ROPE), lambda i, j: (0, i, 0)),
        ],
        out_shape=[
            jax.ShapeDtypeStruct((MLA_HEADS, m, QK_NOPE), BF16),
            jax.ShapeDtypeStruct((MLA_HEADS, m, QK_ROPE), BF16),
        ],
        scratch_shapes=[pltpu.VMEM((TM, Q_LORA), BF16)],
        compiler_params=_cparams(2),
        name=name,
    )(z, g_q, w_q, cos_q, sin_q)


def _kv_post_kernel(zkv_ref, zpe_ref, g_ref, cos_ref, sin_ref, ckv_ref, kpe_ref):
    ckv_ref[...] = _rms(zkv_ref[...], g_ref[...])
    zpe = zpe_ref[...]
    kpe_ref[...] = zpe[:, :QK_ROPE] * cos_ref[...] + zpe[:, QK_ROPE:2 * QK_ROPE] * sin_ref[...]


def _kv_post(z, g_kv, cos_k, sin_k, name):
    m = z.shape[0]
    n_tab = cos_k.shape[0] // TM
    return pl.pallas_call(
        _kv_post_kernel,
        grid=(m // TM,),
        in_specs=[
            pl.BlockSpec((TM, KV_LORA), lambda i: (i, Q_LORA // KV_LORA)),
            pl.BlockSpec((TM, KV_LORA), lambda i: (i, Q_LORA // KV_LORA + 1)),
            pl.BlockSpec((1, KV_LORA), lambda i: (0, 0)),
            pl.BlockSpec((TM, QK_ROPE), lambda i: (i % n_tab, 0)),
            pl.BlockSpec((TM, QK_ROPE), lambda i: (i % n_tab, 0)),
        ],
        out_specs=[
            pl.BlockSpec((TM, KV_LORA), lambda i: (i, 0)),
            pl.BlockSpec((TM, QK_ROPE), lambda i: (i, 0)),
        ],
        out_shape=[
            jax.ShapeDtypeStruct((m, KV_LORA), F32),
            jax.ShapeDtypeStruct((m, QK_ROPE), F32),
        ],
        compiler_params=_cparams(1),
        name=name,
    )(z, z, g_kv, cos_k, sin_k)


def _kv_up_kernel(c_ref, w_ref, kn_ref, v_ref):
    acc = jnp.dot(c_ref[...].astype(BF16), w_ref[...], preferred_element_type=F32)
    j = pl.program_id(1)

    @pl.when(j == 0)
    def _():
        for h in range(MLA_HEADS):
            kn_ref[h] = acc[:, h * QK_NOPE:(h + 1) * QK_NOPE].astype(BF16)

    @pl.when(j == 1)
    def _():
        for h in range(MLA_HEADS):
            v_ref[h] = acc[:, h * V_HEAD:(h + 1) * V_HEAD].astype(BF16)


def _kv_up(ckv, w_kv):
    m = ckv.shape[0]
    return pl.pallas_call(
        _kv_up_kernel,
        grid=(m // TM, 2),
        in_specs=[
            pl.BlockSpec((TM, KV_LORA), lambda i, j: (i, 0)),
            pl.BlockSpec((KV_LORA, MLA_HEADS * QK_NOPE), lambda i, j: (0, j)),
        ],
        out_specs=[
            pl.BlockSpec((MLA_HEADS, TM, QK_NOPE), lambda i, j: (0, i, 0)),
            pl.BlockSpec((MLA_HEADS, TM, V_HEAD), lambda i, j: (0, i, 0)),
        ],
        out_shape=[
            jax.ShapeDtypeStruct((MLA_HEADS, m, QK_NOPE), BF16),
            jax.ShapeDtypeStruct((MLA_HEADS, m, V_HEAD), BF16),
        ],
        compiler_params=_cparams(2),
        name="kv_up",
    )(ckv, w_kv)


def _prompt_attn_kernel(qn_ref, qp_ref, kn_ref, kpe_ref, v_ref, o_ref):
    k_cat = jnp.concatenate([kn_ref[...], kpe_ref[...].astype(BF16)], axis=-1)
    for b in range(SEQ // ATT_QB):
        rows = slice(b * ATT_QB, (b + 1) * ATT_QB)
        n_keys = (b + 1) * ATT_QB
        q = jnp.concatenate([qn_ref[rows, :], qp_ref[rows, :]], axis=-1)
        s = lax.dot_general(q, k_cat[:n_keys], (((1,), (1,)), ((), ())),
                            preferred_element_type=F32) * SM_SCALE
        q_pos = b * ATT_QB + lax.broadcasted_iota(jnp.int32, s.shape, 0)
        k_pos = lax.broadcasted_iota(jnp.int32, s.shape, 1)
        s = jnp.where(k_pos <= q_pos, s, -jnp.inf)
        p = jnp.exp(s - jnp.max(s, axis=-1, keepdims=True))
        l = jnp.sum(p, axis=-1, keepdims=True)
        o = jnp.dot(p.astype(BF16), v_ref[:n_keys, :], preferred_element_type=F32)
        o_ref[rows, :] = o / l


def _prompt_attn(qn, qp, kn, kpe, v):
    n_seq = qn.shape[1] // SEQ
    return pl.pallas_call(
        _prompt_attn_kernel,
        grid=(n_seq, MLA_HEADS),
        in_specs=[
            pl.BlockSpec((None, SEQ, QK_NOPE), lambda n, h: (h, n, 0)),
            pl.BlockSpec((None, SEQ, QK_ROPE), lambda n, h: (h, n, 0)),
            pl.BlockSpec((None, SEQ, QK_NOPE), lambda n, h: (h, n, 0)),
            pl.BlockSpec((SEQ, QK_ROPE), lambda n, h: (n, 0)),
            pl.BlockSpec((None, SEQ, V_HEAD), lambda n, h: (h, n, 0)),
        ],
        out_specs=pl.BlockSpec((SEQ, V_HEAD), lambda n, h: (n, h)),
        out_shape=jax.ShapeDtypeStruct((n_seq * SEQ, ATTN_WIDTH), F32),
        compiler_params=_cparams(2),
        name="prompt_attn",
    )(qn, qp, kn, kpe, v)


def _q_lat_kernel(qn_ref, w_ref, o_ref):
    acc = jnp.dot(qn_ref[...], w_ref[...], preferred_element_type=F32)
    o_ref[...] = acc.reshape(o_ref.shape)


def _q_lat(qn_s, w_ukt):
    ms = qn_s.shape[1]
    nb = ms // DEC_SEQ
    return pl.pallas_call(
        _q_lat_kernel,
        grid=(MLA_HEADS,),
        in_specs=[
            pl.BlockSpec((None, ms, QK_NOPE), lambda h: (h, 0, 0)),
            pl.BlockSpec((None, QK_NOPE, KV_LORA), lambda h: (h, 0, 0)),
        ],
        out_specs=pl.BlockSpec((nb, DEC_SEQ, KV_LORA), lambda h: (0, h, 0)),
        out_shape=jax.ShapeDtypeStruct((nb, MLA_HEADS * DEC_SEQ, KV_LORA), F32),
        compiler_params=_cparams(1),
        name="q_lat",
    )(qn_s, w_ukt)


def _paged_attn_kernel(pt_ref, qlat_ref, qpe_ref, cnew_ref, knew_ref, *rest):
    ckv_hbm, kpet_hbm, o_ref, cbuf, pbuf, sems, ql_scr, kb_scr, m_scr, l_scr, acc_scr = rest
    n = pl.program_id(0)
    n_groups = pt_ref.shape[1] // PAGE_GROUP
    total = pl.num_programs(0) * n_groups
    per = PAGE_GROUP // ATT_CHAINS
    nt = (((1,), (1,)), ((), ()))

    def group_copies(t):
        seq = lax.div(t, n_groups)
        first = lax.rem(t, n_groups) * PAGE_GROUP
        slot = lax.rem(t, PAGE_SLOTS)
        copies = []
        for k in range(PAGE_GROUP):
            page = pt_ref[seq, first + k]
            copies.append(pltpu.make_async_copy(ckv_hbm.at[0, page], cbuf.at[slot, k], sems.at[0, slot]))
            copies.append(pltpu.make_async_copy(kpet_hbm.at[0, page], pbuf.at[slot, k], sems.at[1, slot]))
        return copies

    def start_group(t):
        for idx, cp in enumerate(group_copies(t)):
            cp.start(priority=(idx // 2) % 2)

    def wait_group(t):
        for cp in group_copies(t):
            cp.wait()

    def online(c, s, values):
        m_old = m_scr[c]
        m_new = jnp.maximum(m_old, jnp.max(s, axis=-1, keepdims=True))
        alpha = jnp.exp(m_old - m_new)
        p = jnp.exp(s - m_new)
        l_scr[c] = alpha * l_scr[c] + jnp.sum(p, axis=-1, keepdims=True)
        acc_scr[c] = alpha * acc_scr[c] + jnp.dot(p.astype(BF16), values, preferred_element_type=F32)
        m_scr[c] = m_new

    @pl.when(n == 0)
    def _():
        for t0 in range(PAGE_SLOTS - 1):
            start_group(jnp.int32(t0))

    ql_scr[...] = qlat_ref[...].astype(BF16)
    m_scr[...] = jnp.full(m_scr.shape, -jnp.inf, F32)
    l_scr[...] = jnp.zeros(l_scr.shape, F32)
    acc_scr[...] = jnp.zeros(acc_scr.shape, F32)
    c_new = cnew_ref[...].astype(BF16)
    s_new = (lax.dot_general(ql_scr[...], c_new, nt, preferred_element_type=F32)
             + lax.dot_general(qpe_ref[...], knew_ref[...].astype(BF16), nt, preferred_element_type=F32)) * SM_SCALE
    q_pos = lax.broadcasted_iota(jnp.int32, s_new.shape, 0) % DEC_SEQ
    k_pos = lax.broadcasted_iota(jnp.int32, s_new.shape, 1)
    online(0, jnp.where(k_pos <= q_pos, s_new, -jnp.inf), c_new)

    def group_step(g, carry):
        t = n * n_groups + g
        slot = lax.rem(t, PAGE_SLOTS)
        wait_group(t)

        @pl.when(t + (PAGE_SLOTS - 1) < total)
        def _():
            start_group(t + (PAGE_SLOTS - 1))

        for c in range(ATT_CHAINS):
            lo = c * per * PAGE_SIZE
            for k in range(per):
                kb_scr[lo + k * PAGE_SIZE:lo + (k + 1) * PAGE_SIZE, :] = cbuf[slot, c * per + k].astype(BF16)
            kb = kb_scr[lo:lo + per * PAGE_SIZE, :]
            kpt = jnp.concatenate([pbuf[slot, c * per + k].astype(BF16) for k in range(per)], axis=-1)
            s = (lax.dot_general(ql_scr[...], kb, nt, preferred_element_type=F32)
                 + jnp.dot(qpe_ref[...], kpt, preferred_element_type=F32)) * SM_SCALE
            online(c, s, kb)
        return carry

    lax.fori_loop(0, n_groups, group_step, 0)

    m = m_scr[0]
    for c in range(1, ATT_CHAINS):
        m = jnp.maximum(m, m_scr[c])
    l = jnp.zeros(m.shape, F32)
    acc = jnp.zeros(acc_scr.shape[1:], F32)
    for c in range(ATT_CHAINS):
        w = jnp.exp(m_scr[c] - m)
        l = l + w * l_scr[c]
        acc = acc + w * acc_scr[c]
    o_ref[...] = acc / l


def _paged_attn(page_table, q_lat, q_pe, ckv_new, kpe_new, cache_ckv, cache_kpe_t):
    nb = q_lat.shape[0]
    rows = MLA_HEADS * DEC_SEQ
    n_pages = page_table.shape[1]
    assert n_pages % PAGE_GROUP == 0 and nb * (n_pages // PAGE_GROUP) >= PAGE_SLOTS - 1

    grid_spec = pltpu.PrefetchScalarGridSpec(
        num_scalar_prefetch=1,
        grid=(nb,),
        in_specs=[
            pl.BlockSpec((None, rows, KV_LORA), lambda n, pt: (n, 0, 0)),
            pl.BlockSpec((None, rows, QK_ROPE), lambda n, pt: (n, 0, 0)),
            pl.BlockSpec((DEC_SEQ, KV_LORA), lambda n, pt: (n, 0)),
            pl.BlockSpec((DEC_SEQ, QK_ROPE), lambda n, pt: (n, 0)),
            pl.BlockSpec(memory_space=pl.ANY),
            pl.BlockSpec(memory_space=pl.ANY),
        ],
        out_specs=pl.BlockSpec((None, rows, KV_LORA), lambda n, pt: (n, 0, 0)),
        scratch_shapes=[
            pltpu.VMEM((PAGE_SLOTS, PAGE_GROUP, PAGE_SIZE, KV_LORA), F32),
            pltpu.VMEM((PAGE_SLOTS, PAGE_GROUP, QK_ROPE, PAGE_SIZE), F32),
            pltpu.SemaphoreType.DMA((2, PAGE_SLOTS)),
            pltpu.VMEM((rows, KV_LORA), BF16),
            pltpu.VMEM((PAGE_GROUP * PAGE_SIZE, KV_LORA), BF16),
            pltpu.VMEM((ATT_CHAINS, rows, 1), F32),
            pltpu.VMEM((ATT_CHAINS, rows, 1), F32),
            pltpu.VMEM((ATT_CHAINS, rows, KV_LORA), F32),
        ],
    )
    return pl.pallas_call(
        _paged_attn_kernel,
        grid_spec=grid_spec,
        out_shape=jax.ShapeDtypeStruct((nb, rows, KV_LORA), F32),
        compiler_params=_cparams(1),
        name="paged_attn",
    )(page_table, q_lat, q_pe, ckv_new, kpe_new, cache_ckv, cache_kpe_t)


def _o_up_kernel(o_ref, w_ref, out_ref):
    o = o_ref[...]
    o = o.reshape(o.shape[0] * o.shape[1], o.shape[2]).astype(BF16)
    out_ref[...] = jnp.dot(o, w_ref[...], preferred_element_type=F32)


def _o_up(o_lat, w_uvh):
    nb = o_lat.shape[0]
    ms = nb * DEC_SEQ
    return pl.pallas_call(
        _o_up_kernel,
        grid=(MLA_HEADS,),
        in_specs=[
            pl.BlockSpec((nb, DEC_SEQ, KV_LORA), lambda h: (0, h, 0)),
            pl.BlockSpec((None, KV_LORA, V_HEAD), lambda h: (h, 0, 0)),
        ],
        out_specs=pl.BlockSpec((ms, V_HEAD), lambda h: (0, h)),
        out_shape=jax.ShapeDtypeStruct((ms, ATTN_WIDTH), F32),
        compiler_params=_cparams(1),
        name="o_up",
    )(o_lat, w_uvh)


def _lru_gates(xc, wax_ref, ba, bx, sp):
    a_parts, b_parts = [], []
    for blk in range(LRU_HEADS):
        sl = slice(blk * LRU_BLOCK, (blk + 1) * LRU_BLOCK)
        xb = xc[:, sl]
        y = jnp.dot(xb.astype(BF16), wax_ref[blk], preferred_element_type=F32)
        r = jax.nn.sigmoid(y[:, :LRU_BLOCK] + ba[:, sl])
        i_gate = jax.nn.sigmoid(y[:, LRU_BLOCK:] + bx[:, sl])
        a = jnp.exp(-LRU_C * r * sp[:, sl])
        a_parts.append(a)
        b_parts.append(jnp.sqrt(1.0 - a * a) * (i_gate * xb))
    return jnp.concatenate(a_parts, axis=-1), jnp.concatenate(b_parts, axis=-1)


def _lru_prompt_kernel(xr_ref, yr_ref, cw_ref, cb_ref, wax_ref, ba_ref, bx_ref, lam_ref,
                       o_ref, hl_ref, nb_ref, xext, a_scr, b_scr, hs_scr, h_carry):
    c = pl.program_id(1)
    tc = xr_ref.shape[0]
    pad = SUBLANES

    @pl.when(c == 0)
    def _():
        xext[0:pad, :] = jnp.zeros((pad, LRU_WIDTH), F32)
        h_carry[...] = jnp.zeros(h_carry.shape, F32)

    @pl.when(c > 0)
    def _():
        xext[0:pad, :] = xext[tc:tc + pad, :]

    xr = xr_ref[...]
    xext[pad:, :] = xr
    cw = cw_ref[...]
    xc = cb_ref[...]
    for j in range(LRU_CONV - 1):
        xc = xc + cw[j:j + 1, :] * xext[pl.ds(pad - (LRU_CONV - 1) + j, tc), :]
    xc = xc + cw[LRU_CONV - 1:LRU_CONV, :] * xr

    sp = jax.nn.softplus(-lam_ref[...])
    a, b = _lru_gates(xc, wax_ref, ba_ref[...], bx_ref[...], sp)
    a_scr[...] = a
    b_scr[...] = b

    def step(t, h):
        h = a_scr[pl.ds(t, 1), :] * h + b_scr[pl.ds(t, 1), :]
        hs_scr[pl.ds(t, 1), :] = h
        return h

    h = lax.fori_loop(0, tc, step, h_carry[...], unroll=8)
    h_carry[...] = h
    o_ref[...] = hs_scr[...] * jax.nn.gelu(yr_ref[...])
    hl_ref[...] = h
    nb_ref[...] = xext[tc + pad - (LRU_CONV - 1):tc + pad, :]


def _lru_prompt(z, cw, cb, wax, ba, bx, lam):
    m = z.shape[0]
    n_seq = m // SEQ
    n_chunk = SEQ // LRU_TC
    row = lambda: pl.BlockSpec((1, LRU_WIDTH), lambda n, c: (0, 0))
    return pl.pallas_call(
        _lru_prompt_kernel,
        grid=(n_seq, n_chunk),
        in_specs=[
            pl.BlockSpec((LRU_TC, LRU_WIDTH), lambda n, c: (n * n_chunk + c, 1)),
            pl.BlockSpec((LRU_TC, LRU_WIDTH), lambda n, c: (n * n_chunk + c, 2)),
            pl.BlockSpec((LRU_CONV, LRU_WIDTH), lambda n, c: (0, 0)),
            row(),
            pl.BlockSpec((LRU_HEADS, LRU_BLOCK, 2 * LRU_BLOCK), lambda n, c: (0, 0, 0)),
            row(), row(), row(),
        ],
        out_specs=[
            pl.BlockSpec((LRU_TC, LRU_WIDTH), lambda n, c: (n * n_chunk + c, 0)),
            pl.BlockSpec((None, 1, LRU_WIDTH), lambda n, c: (n, 0, 0)),
            pl.BlockSpec((None, LRU_CONV - 1, LRU_WIDTH), lambda n, c: (n, 0, 0)),
        ],
        out_shape=[
            jax.ShapeDtypeStruct((m, LRU_WIDTH), F32),
            jax.ShapeDtypeStruct((n_seq, 1, LRU_WIDTH), F32),
            jax.ShapeDtypeStruct((n_seq, LRU_CONV - 1, LRU_WIDTH), F32),
        ],
        scratch_shapes=[
            pltpu.VMEM((LRU_TC + SUBLANES, LRU_WIDTH), F32),
            pltpu.VMEM((LRU_TC, LRU_WIDTH), F32),
            pltpu.VMEM((LRU_TC, LRU_WIDTH), F32),
            pltpu.VMEM((LRU_TC, LRU_WIDTH), F32),
            pltpu.VMEM((1, LRU_WIDTH), F32),
        ],
        compiler_params=_cparams(2),
        name="lru_prompt",
    )(z, z, cw, cb, wax, ba, bx, lam)


def _lru_sample_kernel(xr_ref, yr_ref, buf_ref, h0_ref, cw_ref, cb_ref, wax_ref, ba_ref, bx_ref, lam_ref,
                       o_ref, hl_ref, nb_ref):
    rows = xr_ref.shape[0]
    sb = rows // DEC_SEQ
    x3 = xr_ref[...].reshape(sb, DEC_SEQ, LRU_WIDTH)
    ext = jnp.concatenate([buf_ref[...], x3], axis=1)
    cw = cw_ref[...]
    xc3 = cb_ref[...][None]
    for j in range(LRU_CONV):
        xc3 = xc3 + cw[j:j + 1, :][None] * ext[:, j:j + DEC_SEQ, :]
    xc = xc3.reshape(rows, LRU_WIDTH)

    sp = jax.nn.softplus(-lam_ref[...])
    a, b = _lru_gates(xc, wax_ref, ba_ref[...], bx_ref[...], sp)
    a3 = a.reshape(sb, DEC_SEQ, LRU_WIDTH)
    b3 = b.reshape(sb, DEC_SEQ, LRU_WIDTH)
    h = h0_ref[...]
    hs = []
    for t in range(DEC_SEQ):
        h = a3[:, t:t + 1, :] * h + b3[:, t:t + 1, :]
        hs.append(h)
    hs3 = jnp.concatenate(hs, axis=1)
    o_ref[...] = hs3.reshape(rows, LRU_WIDTH) * jax.nn.gelu(yr_ref[...])
    hl_ref[...] = h
    nb_ref[...] = ext[:, DEC_SEQ:DEC_SEQ + LRU_CONV - 1, :]


def _lru_sample(z, buf, h0, cw, cb, wax, ba, bx, lam):
    m = z.shape[0]
    nb = m // DEC_SEQ
    rows = LRU_SB * DEC_SEQ
    row = lambda: pl.BlockSpec((1, LRU_WIDTH), lambda s: (0, 0))
    return pl.pallas_call(
        _lru_sample_kernel,
        grid=(nb // LRU_SB,),
        in_specs=[
            pl.BlockSpec((rows, LRU_WIDTH), lambda s: (s, 1)),
            pl.BlockSpec((rows, LRU_WIDTH), lambda s: (s, 2)),
            pl.BlockSpec((LRU_SB, LRU_CONV - 1, LRU_WIDTH), lambda s: (s, 0, 0)),
            pl.BlockSpec((LRU_SB, 1, LRU_WIDTH), lambda s: (s, 0, 0)),
            pl.BlockSpec((LRU_CONV, LRU_WIDTH), lambda s: (0, 0)),
            row(),
            pl.BlockSpec((LRU_HEADS, LRU_BLOCK, 2 * LRU_BLOCK), lambda s: (0, 0, 0)),
            row(), row(), row(),
        ],
        out_specs=[
            pl.BlockSpec((rows, LRU_WIDTH), lambda s: (s, 0)),
            pl.BlockSpec((LRU_SB, 1, LRU_WIDTH), lambda s: (s, 0, 0)),
            pl.BlockSpec((LRU_SB, LRU_CONV - 1, LRU_WIDTH), lambda s: (s, 0, 0)),
        ],
        out_shape=[
            jax.ShapeDtypeStruct((m, LRU_WIDTH), F32),
            jax.ShapeDtypeStruct((nb, 1, LRU_WIDTH), F32),
            jax.ShapeDtypeStruct((nb, LRU_CONV - 1, LRU_WIDTH), F32),
        ],
        compiler_params=_cparams(1),
        name="lru_sample",
    )(z, z, buf, h0, cw, cb, wax, ba, bx, lam)


def _ffn_up_kernel(*refs, sample, tiles_per_seq, n_col, n_tiles):
    if sample:
        (h_ref, wg_ref, wv_ref, cwg_ref, cwv_ref, cbg_ref, cbv_ref, stg_ref, stv_ref,
         act_ref, bufg_ref, bufv_ref, ug_scr, uv_scr) = refs
    else:
        (h_ref, wg_ref, wv_ref, cwg_ref, cwv_ref, cbg_ref, cbv_ref,
         act_ref, bufg_ref, bufv_ref, ug_scr, uv_scr, halo_g, halo_v) = refs
    t = pl.program_id(0)
    tm = h_ref.shape[0]
    tf = wg_ref.shape[1]
    pad = SUBLANES
    slot_m = lax.rem(t, 2)
    slot_e = 1 - slot_m

    @pl.when(t == 0)
    def _():
        ug_scr[...] = jnp.zeros(ug_scr.shape, F32)
        uv_scr[...] = jnp.zeros(uv_scr.shape, F32)

    tile = jnp.minimum(t, n_tiles - 1)
    i_m = lax.div(tile, n_col)
    j_m = lax.rem(tile, n_col)
    h = h_ref[...]

    def project(w_ref, u_scr, halo):
        u = jnp.dot(h, w_ref[...], preferred_element_type=F32)
        u_scr[slot_m, pad:, :] = u
        if halo is not None:
            first = lax.rem(i_m, tiles_per_seq) == 0
            u_scr[slot_m, 0:pad, :] = jnp.where(first, 0.0, halo[j_m])
            halo[j_m] = u[tm - pad:, :]

    project(wg_ref, ug_scr, None if sample else halo_g)
    project(wv_ref, uv_scr, None if sample else halo_v)

    def conv_prompt(u_scr, cw_ref, cb_ref, buf_ref):
        u = u_scr[slot_e, pl.ds(pad, tm), :]
        buf_ref[...] = u_scr[slot_e, pl.ds(pad + tm - (FFN_CONV - 1), FFN_CONV - 1), :]
        cw = cw_ref[...]
        out = cb_ref[...]
        for k in range(FFN_CONV - 1):
            out = out + cw[k:k + 1, :] * u_scr[slot_e, pl.ds(pad - (FFN_CONV - 1) + k, tm), :]
        return out + cw[FFN_CONV - 1:FFN_CONV, :] * u

    def conv_sample(u_scr, cw_ref, cb_ref, st_ref, buf_ref):
        u3 = u_scr[slot_e, pl.ds(pad, tm), :].reshape(tm // DEC_SEQ, DEC_SEQ, tf)
        ext = jnp.concatenate([st_ref[...], u3], axis=1)
        buf_ref[...] = u3[:, DEC_SEQ - (FFN_CONV - 1):, :]
        cw = cw_ref[...]
        out = cb_ref[...][None]
        for k in range(FFN_CONV):
            out = out + cw[k:k + 1, :][None] * ext[:, k:k + DEC_SEQ, :]
        return out.reshape(tm, tf)

    if sample:
        gate = conv_sample(ug_scr, cwg_ref, cbg_ref, stg_ref, bufg_ref)
        val = conv_sample(uv_scr, cwv_ref, cbv_ref, stv_ref, bufv_ref)
    else:
        gate = conv_prompt(ug_scr, cwg_ref, cbg_ref, bufg_ref)
        val = conv_prompt(uv_scr, cwv_ref, cbv_ref, bufv_ref)
    act_ref[...] = (jax.nn.silu(gate) * val).astype(BF16)


def _ffn_up(h2, w_up, conv_w, conv_b, state=None):
    m = h2.shape[0]
    sample = state is not None
    tm, tf = FF_TM, FF_TF
    n_col = D_FF // tf
    n_row = m // tm
    n_tiles = n_row * n_col

    def mm(t):
        tile = jnp.minimum(t, n_tiles - 1)
        return tile // n_col, tile % n_col

    def ep(t):
        tile = jnp.maximum(t - 1, 0)
        return tile // n_col, tile % n_col

    def col_specs(shape, which, half):
        off = half * n_col
        if len(shape) == 2:
            return pl.BlockSpec(shape, lambda t: (0, which(t)[1] + off))
        return pl.BlockSpec(shape, lambda t: (which(t)[0], 0, which(t)[1] + off))

    in_specs = [pl.BlockSpec((tm, D_MODEL), lambda t: (mm(t)[0], 0)),
                col_specs((D_MODEL, tf), mm, 0), col_specs((D_MODEL, tf), mm, 1),
                col_specs((FFN_CONV, tf), ep, 0), col_specs((FFN_CONV, tf), ep, 1),
                col_specs((1, tf), ep, 0), col_specs((1, tf), ep, 1)]
    args = [h2, w_up, w_up, conv_w, conv_w, conv_b, conv_b]
    scratch = [pltpu.VMEM((2, tm + SUBLANES, tf), F32), pltpu.VMEM((2, tm + SUBLANES, tf), F32)]
    if sample:
        seqs = tm // DEC_SEQ
        n_buf = m // DEC_SEQ
        in_specs += [col_specs((seqs, FFN_CONV - 1, tf), ep, 0), col_specs((seqs, FFN_CONV - 1, tf), ep, 1)]
        args += [state, state]
        buf_spec = lambda: col_specs((seqs, FFN_CONV - 1, tf), ep, 0)
        tiles_per_seq = 1
    else:
        tiles_per_seq = SEQ // tm
        n_buf = n_row
        buf_spec = lambda: col_specs((None, FFN_CONV - 1, tf), ep, 0)
        scratch += [pltpu.VMEM((n_col, SUBLANES, tf), F32), pltpu.VMEM((n_col, SUBLANES, tf), F32)]
    buf_shape = jax.ShapeDtypeStruct((n_buf, FFN_CONV - 1, D_FF), F32)
    return pl.pallas_call(
        functools.partial(_ffn_up_kernel, sample=sample, tiles_per_seq=tiles_per_seq, n_col=n_col, n_tiles=n_tiles),
        grid=(n_tiles + 1,),
        in_specs=in_specs,
        out_specs=[pl.BlockSpec((tm, tf), lambda t: ep(t)), buf_spec(), buf_spec()],
        out_shape=[jax.ShapeDtypeStruct((m, D_FF), BF16), buf_shape, buf_shape],
        scratch_shapes=scratch,
        compiler_params=_cparams(1),
        name="ffn_up_sample" if sample else "ffn_up_prompt",
    )(*args)


def _ffn_down_kernel(act_ref, w_ref, x_ref, g_ref, o_ref, *, k_last_valid):
    k = pl.program_id(1)
    last = pl.num_programs(1) - 1

    @pl.when(k == 0)
    def _():
        o_ref[...] = x_ref[...]

    def accumulate(k_valid):
        act = act_ref[:, :k_valid]
        for c in range(D_MODEL // FF_DOWN_CHUNK):
            cols = slice(c * FF_DOWN_CHUNK, (c + 1) * FF_DOWN_CHUNK)
            o_ref[:, cols] += jnp.dot(act, w_ref[:k_valid, cols], preferred_element_type=F32)

    @pl.when(k < last)
    def _():
        accumulate(FF_TK)

    @pl.when(k == last)
    def _():
        accumulate(k_last_valid)
        for r in range(0, o_ref.shape[0], FF_NORM_ROWS):
            o_ref[r:r + FF_NORM_ROWS, :] = _rms(o_ref[r:r + FF_NORM_ROWS, :], g_ref[...])


def _ffn_down(act, w_down, x2, g_final, name):
    m, k_tot = act.shape
    n_k = pl.cdiv(k_tot, FF_TK)
    return pl.pallas_call(
        functools.partial(_ffn_down_kernel, k_last_valid=k_tot - (n_k - 1) * FF_TK),
        grid=(m // TM, n_k),
        in_specs=[
            pl.BlockSpec((TM, FF_TK), lambda i, k: (i, k)),
            pl.BlockSpec((FF_TK, D_MODEL), lambda i, k: (k, 0)),
            pl.BlockSpec((TM, D_MODEL), lambda i, k: (i, 0)),
            pl.BlockSpec((1, D_MODEL), lambda i, k: (0, 0)),
        ],
        out_specs=pl.BlockSpec((TM, D_MODEL), lambda i, k: (i, 0)),
        out_shape=jax.ShapeDtypeStruct((m, D_MODEL), F32),
        compiler_params=_cparams(2),
        name=name,
    )(act, w_down, x2, g_final)


def _swap_pairs(w):
    shp = w.shape
    return w.reshape(shp[:-1] + (shp[-1] // 2, 2))[..., ::-1].reshape(shp)


def _rope_tables(pos):
    half = QK_ROPE // 2
    freqs = ROPE_THETA ** (-jnp.arange(half, dtype=F32) * 2.0 / QK_ROPE)
    ang = pos.astype(F32)[:, None] * freqs
    cos, sin = jnp.cos(ang), jnp.sin(ang)
    cos_t = jnp.repeat(cos, 2, axis=-1)
    sin_t = jnp.stack([-sin, sin], axis=-1).reshape(pos.shape[0], QK_ROPE)
    return cos_t, sin_t


def kernel(x_prompt, x_sample, cache_ckv, cache_kpe, page_table, state_lru_h, state_lru_conv, state_ffn_conv, g_mix, w_in, g_q, w_uq, g_kv, w_uk, w_uv, lru_conv_w, lru_conv_b, lru_w_a, lru_b_a, lru_w_x, lru_b_x, lru_lambda, g_attn_out, g_lru_out, w_o, g_ffn, w_up, ffn_conv_w, ffn_conv_b, w_down, g_final):
    assert w_in.shape[0] == 1, "single-layer trunk"
    n_p, n_s = x_prompt.shape[0], x_sample.shape[0]
    row = lambda v: v.reshape(1, -1)

    wi = w_in[0]
    c0, c1, c2, c3 = Q_LORA, Q_LORA + KV_LORA, Q_LORA + KV_LORA + QK_ROPE, Q_LORA + KV_LORA + QK_ROPE + LRU_WIDTH
    w_kpe = wi[:, c1:c2]
    zpad = jnp.zeros((D_MODEL, KV_LORA - 2 * QK_ROPE), F32)
    w_in_p = jnp.concatenate([wi[:, :c1], w_kpe, _swap_pairs(w_kpe), zpad, wi[:, c2:c3], wi[:, c3:]],
                             axis=1).astype(BF16)
    wq = w_uq[0]
    wq_rope = wq[:, :, QK_NOPE:]
    w_q_p = jnp.concatenate([wq[:, :, :QK_NOPE].reshape(Q_LORA, -1), wq_rope.reshape(Q_LORA, -1),
                             _swap_pairs(wq_rope).reshape(Q_LORA, -1)], axis=1).astype(BF16)
    w_kv = jnp.concatenate([w_uk[0].reshape(KV_LORA, -1), w_uv[0].reshape(KV_LORA, -1)], axis=1).astype(BF16)
    w_ukt = jnp.transpose(w_uk[0], (1, 2, 0)).astype(BF16)
    w_uvh = jnp.transpose(w_uv[0], (1, 0, 2)).astype(BF16)
    wax = jnp.concatenate([lru_w_a[0], lru_w_x[0]], axis=-1).astype(BF16)
    w_o_b = w_o[0].astype(BF16)
    w_up_b = w_up[0].astype(BF16)
    w_down_b = w_down[0].astype(BF16)

    cos_p, sin_p = _rope_tables(jnp.arange(SEQ))
    cos_s, sin_s = _rope_tables(PAST_LEN + jnp.arange(DEC_SEQ))
    cos_s, sin_s = jnp.tile(cos_s, (n_s, 1)), jnp.tile(sin_s, (n_s, 1))

    lru_args = (lru_conv_w[0], row(lru_conv_b[0]), wax, row(lru_b_a[0]), row(lru_b_x[0]), row(lru_lambda[0]))

    def trunk(x, cos_k, sin_k, tag):
        z = _norm_matmul([(x, 0, D_MODEL, row(g_mix[0]))], w_in_p, None, IN_TN, F32, "in_proj_" + tag)
        qn, qp = _q_proj(z, row(g_q[0]), w_q_p, jnp.tile(cos_k, (1, MLA_HEADS)), jnp.tile(sin_k, (1, MLA_HEADS)),
                         "q_proj_" + tag)
        ckv, kpe = _kv_post(z, row(g_kv[0]), cos_k, sin_k, "kv_post_" + tag)
        return z, qn, qp, ckv, kpe

    def tail(x, o_attn, o_lru, tag, state=None):
        x2 = _norm_matmul([(o_attn, 0, ATTN_WIDTH, row(g_attn_out[0])), (o_lru, 0, LRU_WIDTH, row(g_lru_out[0]))],
                          w_o_b, x, IN_TN, F32, "out_proj_" + tag)
        h2 = _norm_cast(x2, row(g_ffn[0]), "ffn_norm_" + tag)
        act, buf_g, buf_v = _ffn_up(h2, w_up_b, ffn_conv_w[0], row(ffn_conv_b[0]), state)
        y = _ffn_down(act, w_down_b, x2, row(g_final), "ffn_down_" + tag)
        if state is None:
            per_seq = SEQ // FF_TM
            buf_g, buf_v = buf_g[per_seq - 1::per_seq], buf_v[per_seq - 1::per_seq]
        new_buf = jnp.concatenate([buf_g, buf_v], axis=-1)
        return y, new_buf

    xp = x_prompt.reshape(n_p * SEQ, D_MODEL)
    z_p, qn_p, qp_p, ckv_p, kpe_p = trunk(xp, cos_p, sin_p, "p")
    kn_p, v_p = _kv_up(ckv_p, w_kv)
    o_attn_p = _prompt_attn(qn_p, qp_p, kn_p, kpe_p, v_p)
    o_lru_p, hl_p, lbuf_p = _lru_prompt(z_p, *lru_args)
    y_p, fbuf_p = tail(xp, o_attn_p, o_lru_p, "p")

    xs = x_sample.reshape(n_s * DEC_SEQ, D_MODEL)
    z_s, qn_s, qp_s, ckv_s, kpe_s = trunk(xs, cos_s, sin_s, "s")
    q_lat = _q_lat(qn_s, w_ukt)
    q_pe = qp_s.reshape(MLA_HEADS, n_s, DEC_SEQ, QK_ROPE).transpose(1, 0, 2, 3).reshape(
        n_s, MLA_HEADS * DEC_SEQ, QK_ROPE)
    o_lat = _paged_attn(page_table, q_lat, q_pe, ckv_s, kpe_s, cache_ckv, jnp.swapaxes(cache_kpe, 2, 3))
    o_attn_s = _o_up(o_lat, w_uvh)
    o_lru_s, hl_s, lbuf_s = _lru_sample(z_s, state_lru_conv[0], state_lru_h[0].reshape(n_s, 1, LRU_WIDTH), *lru_args)
    y_s, fbuf_s = tail(xs, o_attn_s, o_lru_s, "s", state_ffn_conv[0])

    return (
        y_p.reshape(n_p, SEQ, D_MODEL),
        y_s.reshape(n_s, DEC_SEQ, D_MODEL),
        ckv_p.reshape(1, n_p, SEQ, KV_LORA),
        kpe_p.reshape(1, n_p, SEQ, QK_ROPE),
        ckv_s.reshape(1, n_s, DEC_SEQ, KV_LORA),
        kpe_s.reshape(1, n_s, DEC_SEQ, QK_ROPE),
        hl_p.reshape(1, n_p, LRU_WIDTH),
        hl_s.reshape(1, n_s, LRU_WIDTH),
        lbuf_p[None],
        lbuf_s[None],
        fbuf_p[None],
        fbuf_s[None],
    )
```

```python
import functools

import jax
import jax.numpy as jnp
from jax import lax
from jax.experimental import pallas as pl
from jax.experimental.pallas import tpu as pltpu

F32 = jnp.float32
BF16 = jnp.bfloat16

D_MODEL = 4096
SEQ = 2048
DEC_SEQ = 8
PAST_LEN = 16384
PAGE_SIZE = 128
MLA_HEADS = 16
QK_NOPE = 128
QK_ROPE = 64
V_HEAD = 128
Q_LORA = 1024
KV_LORA = 512
ROPE_THETA = 10000.0
SM_SCALE = (QK_NOPE + QK_ROPE) ** -0.5
LRU_WIDTH = D_MODEL // 2
LRU_HEADS = 16
LRU_BLOCK = LRU_WIDTH // LRU_HEADS
LRU_CONV = 4
LRU_C = 8.0
D_FF = 11008
FFN_CONV = 3
EPS = 1e-6
ATTN_WIDTH = MLA_HEADS * V_HEAD

VMEM_LIMIT_BYTES = 56 * 1024 * 1024
SUBLANES = 8

TM = 512
IN_PAD = 6144
IN_TN = 1024
FF_TM_PROMPT = 2048
FF_TM_SAMPLE = 1024
FF_TF = 256
FF_TK = 1024
FF_DOWN_CHUNK = 1024
FF_NORM_ROWS = 128
LRU_TC = 256
LRU_SB = 32
ATT_QB = 512
PAGE_GROUP = 8
PAGE_SLOTS = 4
ATT_CHAINS = 4


def _cparams(n_axes):
    return pltpu.CompilerParams(dimension_semantics=("arbitrary",) * n_axes,
                                vmem_limit_bytes=VMEM_LIMIT_BYTES)


def _rms(x, g):
    return x * lax.rsqrt(jnp.mean(x * x, axis=-1, keepdims=True) + EPS) * g


def _norm_matmul_kernel(*refs, n_a, has_res):
    a_refs = refs[0:2 * n_a:2]
    g_refs = refs[1:2 * n_a:2]
    w_ref = refs[2 * n_a]
    res_ref = refs[2 * n_a + 1] if has_res else None
    o_ref, h_scr = refs[-2], refs[-1]

    @pl.when(pl.program_id(1) == 0)
    def _():
        off = 0
        for a_ref, g_ref in zip(a_refs, g_refs):
            k = a_ref.shape[-1]
            h_scr[:, off:off + k] = _rms(a_ref[...], g_ref[...]).astype(BF16)
            off += k

    acc = jnp.dot(h_scr[...], w_ref[...], preferred_element_type=F32)
    if has_res:
        acc = acc + res_ref[...]
    o_ref[...] = acc.astype(o_ref.dtype)


def _norm_matmul(a_list, w, res, tn, out_dtype, name):
    m = a_list[0][0].shape[0]
    k_tot, n = w.shape
    in_specs, args = [], []
    for arr, cb, ka, g in a_list:
        in_specs.append(pl.BlockSpec((TM, ka), lambda i, j, cb=cb: (i, cb)))
        in_specs.append(pl.BlockSpec((1, ka), lambda i, j: (0, 0)))
        args += [arr, g]
    in_specs.append(pl.BlockSpec((k_tot, tn), lambda i, j: (0, j)))
    args.append(w)
    if res is not None:
        in_specs.append(pl.BlockSpec((TM, tn), lambda i, j: (i, j)))
        args.append(res)
    return pl.pallas_call(
        functools.partial(_norm_matmul_kernel, n_a=len(a_list), has_res=res is not None),
        grid=(m // TM, n // tn),
        in_specs=in_specs,
        out_specs=pl.BlockSpec((TM, tn), lambda i, j: (i, j)),
        out_shape=jax.ShapeDtypeStruct((m, n), out_dtype),
        scratch_shapes=[pltpu.VMEM((TM, k_tot), BF16)],
        compiler_params=_cparams(2),
        name=name,
    )(*args)


def _norm_cast_kernel(x_ref, g_ref, o_ref):
    o_ref[...] = _rms(x_ref[...], g_ref[...]).astype(BF16)


def _norm_cast(x, g, name):
    m, k = x.shape
    return pl.pallas_call(
        _norm_cast_kernel,
        grid=(m // TM,),
        in_specs=[pl.BlockSpec((TM, k), lambda i: (i, 0)), pl.BlockSpec((1, k), lambda i: (0, 0))],
        out_specs=pl.BlockSpec((TM, k), lambda i: (i, 0)),
        out_shape=jax.ShapeDtypeStruct((m, k), BF16),
        compiler_params=_cparams(1),
        name=name,
    )(x, g)


def _q_proj_kernel(a_ref, g_ref, w_ref, cos_ref, sin_ref, qn_ref, qp_ref, h_scr):
    j = pl.program_id(1)

    @pl.when(j == 0)
    def _():
        h_scr[...] = _rms(a_ref[...], g_ref[...]).astype(BF16)

    acc = jnp.dot(h_scr[...], w_ref[...], preferred_element_type=F32)

    @pl.when(j == 0)
    def _():
        for h in range(MLA_HEADS):
            qn_ref[h] = acc[:, h * QK_NOPE:(h + 1) * QK_NOPE].astype(BF16)

    @pl.when(j == 1)
    def _():
        half = MLA_HEADS * QK_ROPE
        roped = acc[:, :half] * cos_ref[...] + acc[:, half:] * sin_ref[...]
        for h in range(MLA_HEADS):
            qp_ref[h] = roped[:, h * QK_ROPE:(h + 1) * QK_ROPE].astype(BF16)


def _q_proj(z, g_q, w_q, cos_q, sin_q, name):
    m = z.shape[0]
    n_tab = cos_q.shape[0] // TM
    half = MLA_HEADS * QK_ROPE
    return pl.pallas_call(
        _q_proj_kernel,
        grid=(m // TM, 2),
        in_specs=[
            pl.BlockSpec((TM, Q_LORA), lambda i, j: (i, 0)),
            pl.BlockSpec((1, Q_LORA), lambda i, j: (0, 0)),
            pl.BlockSpec((Q_LORA, 2 * half), lambda i, j: (0, j)),
            pl.BlockSpec((TM, half), lambda i, j: (i % n_tab, 0)),
            pl.BlockSpec((TM, half), lambda i, j: (i % n_tab, 0)),
        ],
        out_specs=[
            pl.BlockSpec((MLA_HEADS, TM, QK_NOPE), lambda i, j: (0, i, 0)),
            pl.BlockSpec((MLA_HEADS, TM, QK_ROPE), lambda i, j: (0, i, 0)),
        ],
        out_shape=[
            jax.ShapeDtypeStruct((MLA_HEADS, m, QK_NOPE), BF16),
            jax.ShapeDtypeStruct((MLA_HEADS, m, QK_ROPE), BF16),
        ],
        scratch_shapes=[pltpu.VMEM((TM, Q_LORA), BF16)],
        compiler_params=_cparams(2),
        name=name,
    )(z, g_q, w_q, cos_q, sin_q)


def _kv_post_kernel(zkv_ref, zpe_ref, g_ref, cos_ref, sin_ref, ckv_ref, kpe_ref):
    ckv_ref[...] = _rms(zkv_ref[...], g_ref[...])
    zpe = zpe_ref[...]
    kpe_ref[...] = zpe[:, :QK_ROPE] * cos_ref[...] + zpe[:, QK_ROPE:2 * QK_ROPE] * sin_ref[...]


def _kv_post(z, g_kv, cos_k, sin_k, name):
    m = z.shape[0]
    n_tab = cos_k.shape[0] // TM
    return pl.pallas_call(
        _kv_post_kernel,
        grid=(m // TM,),
        in_specs=[
            pl.BlockSpec((TM, KV_LORA), lambda i: (i, Q_LORA // KV_LORA)),
            pl.BlockSpec((TM, KV_LORA), lambda i: (i, Q_LORA // KV_LORA + 1)),
            pl.BlockSpec((1, KV_LORA), lambda i: (0, 0)),
            pl.BlockSpec((TM, QK_ROPE), lambda i: (i % n_tab, 0)),
            pl.BlockSpec((TM, QK_ROPE), lambda i: (i % n_tab, 0)),
        ],
        out_specs=[
            pl.BlockSpec((TM, KV_LORA), lambda i: (i, 0)),
            pl.BlockSpec((TM, QK_ROPE), lambda i: (i, 0)),
        ],
        out_shape=[
            jax.ShapeDtypeStruct((m, KV_LORA), F32),
            jax.ShapeDtypeStruct((m, QK_ROPE), F32),
        ],
        compiler_params=_cparams(1),
        name=name,
    )(z, z, g_kv, cos_k, sin_k)


def _kv_up_kernel(c_ref, w_ref, kn_ref, v_ref):
    acc = jnp.dot(c_ref[...].astype(BF16), w_ref[...], preferred_element_type=F32)
    j = pl.program_id(1)

    @pl.when(j == 0)
    def _():
        for h in range(MLA_HEADS):
            kn_ref[h] = acc[:, h * QK_NOPE:(h + 1) * QK_NOPE].astype(BF16)

    @pl.when(j == 1)
    def _():
        for h in range(MLA_HEADS):
            v_ref[h] = acc[:, h * V_HEAD:(h + 1) * V_HEAD].astype(BF16)


def _kv_up(ckv, w_kv):
    m = ckv.shape[0]
    return pl.pallas_call(
        _kv_up_kernel,
        grid=(m // TM, 2),
        in_specs=[
            pl.BlockSpec((TM, KV_LORA), lambda i, j: (i, 0)),
            pl.BlockSpec((KV_LORA, MLA_HEADS * QK_NOPE), lambda i, j: (0, j)),
        ],
        out_specs=[
            pl.BlockSpec((MLA_HEADS, TM, QK_NOPE), lambda i, j: (0, i, 0)),
            pl.BlockSpec((MLA_HEADS, TM, V_HEAD), lambda i, j: (0, i, 0)),
        ],
        out_shape=[
            jax.ShapeDtypeStruct((MLA_HEADS, m, QK_NOPE), BF16),
            jax.ShapeDtypeStruct((MLA_HEADS, m, V_HEAD), BF16),
        ],
        compiler_params=_cparams(2),
        name="kv_up",
    )(ckv, w_kv)


def _prompt_attn_kernel(qn_ref, qp_ref, kn_ref, kpe_ref, v_ref, o_ref):
    k_cat = jnp.concatenate([kn_ref[...], kpe_ref[...].astype(BF16)], axis=-1)
    for b in range(SEQ // ATT_QB):
        rows = slice(b * ATT_QB, (b + 1) * ATT_QB)
        n_keys = (b + 1) * ATT_QB
        q = jnp.concatenate([qn_ref[rows, :], qp_ref[rows, :]], axis=-1)
        s = lax.dot_general(q, k_cat[:n_keys], (((1,), (1,)), ((), ())),
                            preferred_element_type=F32) * SM_SCALE
        q_pos = b * ATT_QB + lax.broadcasted_iota(jnp.int32, s.shape, 0)
        k_pos = lax.broadcasted_iota(jnp.int32, s.shape, 1)
        s = jnp.where(k_pos <= q_pos, s, -jnp.inf)
        p = jnp.exp(s - jnp.max(s, axis=-1, keepdims=True))
        l = jnp.sum(p, axis=-1, keepdims=True)
        o = jnp.dot(p.astype(BF16), v_ref[:n_keys, :], preferred_element_type=F32)
        o_ref[rows, :] = o / l


def _prompt_attn(qn, qp, kn, kpe, v):
    n_seq = qn.shape[1] // SEQ
    return pl.pallas_call(
        _prompt_attn_kernel,
        grid=(n_seq, MLA_HEADS),
        in_specs=[
            pl.BlockSpec((None, SEQ, QK_NOPE), lambda n, h: (h, n, 0)),
            pl.BlockSpec((None, SEQ, QK_ROPE), lambda n, h: (h, n, 0)),
            pl.BlockSpec((None, SEQ, QK_NOPE), lambda n, h: (h, n, 0)),
            pl.BlockSpec((SEQ, QK_ROPE), lambda n, h: (n, 0)),
            pl.BlockSpec((None, SEQ, V_HEAD), lambda n, h: (h, n, 0)),
        ],
        out_specs=pl.BlockSpec((SEQ, V_HEAD), lambda n, h: (n, h)),
        out_shape=jax.ShapeDtypeStruct((n_seq * SEQ, ATTN_WIDTH), F32),
        compiler_params=_cparams(2),
        name="prompt_attn",
    )(qn, qp, kn, kpe, v)


def _q_lat_kernel(qn_ref, w_ref, o_ref):
    acc = jnp.dot(qn_ref[...], w_ref[...], preferred_element_type=F32)
    o_ref[...] = acc.reshape(o_ref.shape)


def _q_lat(qn_s, w_ukt):
    ms = qn_s.shape[1]
    nb = ms // DEC_SEQ
    return pl.pallas_call(
        _q_lat_kernel,
        grid=(MLA_HEADS,),
        in_specs=[
            pl.BlockSpec((None, ms, QK_NOPE), lambda h: (h, 0, 0)),
            pl.BlockSpec((None, QK_NOPE, KV_LORA), lambda h: (h, 0, 0)),
        ],
        out_specs=pl.BlockSpec((nb, DEC_SEQ, KV_LORA), lambda h: (0, h, 0)),
        out_shape=jax.ShapeDtypeStruct((nb, MLA_HEADS * DEC_SEQ, KV_LORA), F32),
        compiler_params=_cparams(1),
        name="q_lat",
    )(qn_s, w_ukt)


def _paged_attn_kernel(pt_ref, qlat_ref, qpe_ref, cnew_ref, knew_ref, *rest):
    ckv_hbm, kpet_hbm, o_ref, cbuf, pbuf, sems, ql_scr, kb_scr, m_scr, l_scr, acc_scr = rest
    n = pl.program_id(0)
    n_groups = pt_ref.shape[1] // PAGE_GROUP
    total = pl.num_programs(0) * n_groups
    per = PAGE_GROUP // ATT_CHAINS
    nt = (((1,), (1,)), ((), ()))

    def group_copies(t):
        seq = lax.div(t, n_groups)
        first = lax.rem(t, n_groups) * PAGE_GROUP
        slot = lax.rem(t, PAGE_SLOTS)
        copies = []
        for k in range(PAGE_GROUP):
            page = pt_ref[seq, first + k]
            copies.append(pltpu.make_async_copy(ckv_hbm.at[0, page], cbuf.at[slot, k], sems.at[0, slot]))
            copies.append(pltpu.make_async_copy(kpet_hbm.at[0, page], pbuf.at[slot, k], sems.at[1, slot]))
        return copies

    def start_group(t):
        for idx, cp in enumerate(group_copies(t)):
            cp.start(priority=(idx // 2) % 2)

    def wait_group(t):
        for cp in group_copies(t):
            cp.wait()

    def online(c, s, values):
        m_old = m_scr[c]
        m_new = jnp.maximum(m_old, jnp.max(s, axis=-1, keepdims=True))
        alpha = jnp.exp(m_old - m_new)
        p = jnp.exp(s - m_new)
        l_scr[c] = alpha * l_scr[c] + jnp.sum(p, axis=-1, keepdims=True)
        acc_scr[c] = alpha * acc_scr[c] + jnp.dot(p.astype(BF16), values, preferred_element_type=F32)
        m_scr[c] = m_new

    @pl.when(n == 0)
    def _():
        for t0 in range(PAGE_SLOTS - 1):
            start_group(jnp.int32(t0))

    ql_scr[...] = qlat_ref[...].astype(BF16)
    m_scr[...] = jnp.full(m_scr.shape, -jnp.inf, F32)
    l_scr[...] = jnp.zeros(l_scr.shape, F32)
    acc_scr[...] = jnp.zeros(acc_scr.shape, F32)
    c_new = cnew_ref[...].astype(BF16)
    s_new = (lax.dot_general(ql_scr[...], c_new, nt, preferred_element_type=F32)
             + lax.dot_general(qpe_ref[...], knew_ref[...].astype(BF16), nt, preferred_element_type=F32)) * SM_SCALE
    q_pos = lax.broadcasted_iota(jnp.int32, s_new.shape, 0) % DEC_SEQ
    k_pos = lax.broadcasted_iota(jnp.int32, s_new.shape, 1)
    online(0, jnp.where(k_pos <= q_pos, s_new, -jnp.inf), c_new)

    def group_step(g, carry):
        t = n * n_groups + g
        slot = lax.rem(t, PAGE_SLOTS)
        wait_group(t)

        @pl.when(t + (PAGE_SLOTS - 1) < total)
        def _():
            start_group(t + (PAGE_SLOTS - 1))

        for k in range(PAGE_GROUP):
            kb_scr[k * PAGE_SIZE:(k + 1) * PAGE_SIZE, :] = cbuf[slot, k].astype(BF16)
        scores = []
        for c in range(ATT_CHAINS):
            lo = c * per * PAGE_SIZE
            kpt = jnp.concatenate([pbuf[slot, c * per + k].astype(BF16) for k in range(per)], axis=-1)
            scores.append((lax.dot_general(ql_scr[...], kb_scr[lo:lo + per * PAGE_SIZE, :], nt,
                                           preferred_element_type=F32)
                           + jnp.dot(qpe_ref[...], kpt, preferred_element_type=F32)) * SM_SCALE)
        probs, alphas = [], []
        for c in range(ATT_CHAINS):
            m_old = m_scr[c]
            m_new = jnp.maximum(m_old, jnp.max(scores[c], axis=-1, keepdims=True))
            alpha = jnp.exp(m_old - m_new)
            p = jnp.exp(scores[c] - m_new)
            l_scr[c] = alpha * l_scr[c] + jnp.sum(p, axis=-1, keepdims=True)
            m_scr[c] = m_new
            probs.append(p.astype(BF16))
            alphas.append(alpha)
        for c in range(ATT_CHAINS):
            lo = c * per * PAGE_SIZE
            acc_scr[c] = alphas[c] * acc_scr[c] + jnp.dot(probs[c], kb_scr[lo:lo + per * PAGE_SIZE, :],
                                                          preferred_element_type=F32)
        return carry

    lax.fori_loop(0, n_groups, group_step, 0)

    m = m_scr[0]
    for c in range(1, ATT_CHAINS):
        m = jnp.maximum(m, m_scr[c])
    l = jnp.zeros(m.shape, F32)
    acc = jnp.zeros(acc_scr.shape[1:], F32)
    for c in range(ATT_CHAINS):
        w = jnp.exp(m_scr[c] - m)
        l = l + w * l_scr[c]
        acc = acc + w * acc_scr[c]
    o_ref[...] = acc / l


def _paged_attn(page_table, q_lat, q_pe, ckv_new, kpe_new, cache_ckv, cache_kpe_t):
    nb = q_lat.shape[0]
    rows = MLA_HEADS * DEC_SEQ
    n_pages = page_table.shape[1]
    assert n_pages % PAGE_GROUP == 0 and nb * (n_pages // PAGE_GROUP) >= PAGE_SLOTS - 1

    grid_spec = pltpu.PrefetchScalarGridSpec(
        num_scalar_prefetch=1,
        grid=(nb,),
        in_specs=[
            pl.BlockSpec((None, rows, KV_LORA), lambda n, pt: (n, 0, 0)),
            pl.BlockSpec((None, rows, QK_ROPE), lambda n, pt: (n, 0, 0)),
            pl.BlockSpec((DEC_SEQ, KV_LORA), lambda n, pt: (n, 0)),
            pl.BlockSpec((DEC_SEQ, QK_ROPE), lambda n, pt: (n, 0)),
            pl.BlockSpec(memory_space=pl.ANY),
            pl.BlockSpec(memory_space=pl.ANY),
        ],
        out_specs=pl.BlockSpec((None, rows, KV_LORA), lambda n, pt: (n, 0, 0)),
        scratch_shapes=[
            pltpu.VMEM((PAGE_SLOTS, PAGE_GROUP, PAGE_SIZE, KV_LORA), F32),
            pltpu.VMEM((PAGE_SLOTS, PAGE_GROUP, QK_ROPE, PAGE_SIZE), F32),
            pltpu.SemaphoreType.DMA((2, PAGE_SLOTS)),
            pltpu.VMEM((rows, KV_LORA), BF16),
            pltpu.VMEM((PAGE_GROUP * PAGE_SIZE, KV_LORA), BF16),
            pltpu.VMEM((ATT_CHAINS, rows, 1), F32),
            pltpu.VMEM((ATT_CHAINS, rows, 1), F32),
            pltpu.VMEM((ATT_CHAINS, rows, KV_LORA), F32),
        ],
    )
    return pl.pallas_call(
        _paged_attn_kernel,
        grid_spec=grid_spec,
        out_shape=jax.ShapeDtypeStruct((nb, rows, KV_LORA), F32),
        compiler_params=_cparams(1),
        name="paged_attn",
    )(page_table, q_lat, q_pe, ckv_new, kpe_new, cache_ckv, cache_kpe_t)


def _o_up_kernel(o_ref, w_ref, out_ref):
    o = o_ref[...]
    o = o.reshape(o.shape[0] * o.shape[1], o.shape[2]).astype(BF16)
    out_ref[...] = jnp.dot(o, w_ref[...], preferred_element_type=F32)


def _o_up(o_lat, w_uvh):
    nb = o_lat.shape[0]
    ms = nb * DEC_SEQ
    return pl.pallas_call(
        _o_up_kernel,
        grid=(MLA_HEADS,),
        in_specs=[
            pl.BlockSpec((nb, DEC_SEQ, KV_LORA), lambda h: (0, h, 0)),
            pl.BlockSpec((None, KV_LORA, V_HEAD), lambda h: (h, 0, 0)),
        ],
        out_specs=pl.BlockSpec((ms, V_HEAD), lambda h: (0, h)),
        out_shape=jax.ShapeDtypeStruct((ms, ATTN_WIDTH), F32),
        compiler_params=_cparams(1),
        name="o_up",
    )(o_lat, w_uvh)


def _lru_gates(xc, wax_ref, ba, bx, sp):
    a_parts, b_parts = [], []
    for blk in range(LRU_HEADS):
        sl = slice(blk * LRU_BLOCK, (blk + 1) * LRU_BLOCK)
        xb = xc[:, sl]
        y = jnp.dot(xb.astype(BF16), wax_ref[blk], preferred_element_type=F32)
        r = jax.nn.sigmoid(y[:, :LRU_BLOCK] + ba[:, sl])
        i_gate = jax.nn.sigmoid(y[:, LRU_BLOCK:] + bx[:, sl])
        a = jnp.exp(-LRU_C * r * sp[:, sl])
        a_parts.append(a)
        b_parts.append(jnp.sqrt(1.0 - a * a) * (i_gate * xb))
    return jnp.concatenate(a_parts, axis=-1), jnp.concatenate(b_parts, axis=-1)


def _lru_prompt_kernel(xr_ref, yr_ref, cw_ref, cb_ref, wax_ref, ba_ref, bx_ref, lam_ref,
                       o_ref, hl_ref, nb_ref, xext, a_scr, b_scr, hs_scr, h_carry):
    c = pl.program_id(1)
    tc = xr_ref.shape[0]
    pad = SUBLANES

    @pl.when(c == 0)
    def _():
        xext[0:pad, :] = jnp.zeros((pad, LRU_WIDTH), F32)
        h_carry[...] = jnp.zeros(h_carry.shape, F32)

    @pl.when(c > 0)
    def _():
        xext[0:pad, :] = xext[tc:tc + pad, :]

    xr = xr_ref[...]
    xext[pad:, :] = xr
    cw = cw_ref[...]
    xc = cb_ref[...]
    for j in range(LRU_CONV - 1):
        xc = xc + cw[j:j + 1, :] * xext[pl.ds(pad - (LRU_CONV - 1) + j, tc), :]
    xc = xc + cw[LRU_CONV - 1:LRU_CONV, :] * xr

    sp = jax.nn.softplus(-lam_ref[...])
    a, b = _lru_gates(xc, wax_ref, ba_ref[...], bx_ref[...], sp)
    a_scr[...] = a
    b_scr[...] = b

    def step(t, h):
        h = a_scr[pl.ds(t, 1), :] * h + b_scr[pl.ds(t, 1), :]
        hs_scr[pl.ds(t, 1), :] = h
        return h

    h = lax.fori_loop(0, tc, step, h_carry[...], unroll=8)
    h_carry[...] = h
    o_ref[...] = hs_scr[...] * jax.nn.gelu(yr_ref[...])
    hl_ref[...] = h
    nb_ref[...] = xext[tc + pad - (LRU_CONV - 1):tc + pad, :]


def _lru_prompt(z, cw, cb, wax, ba, bx, lam):
    m = z.shape[0]
    n_seq = m // SEQ
    n_chunk = SEQ // LRU_TC
    row = lambda: pl.BlockSpec((1, LRU_WIDTH), lambda n, c: (0, 0))
    return pl.pallas_call(
        _lru_prompt_kernel,
        grid=(n_seq, n_chunk),
        in_specs=[
            pl.BlockSpec((LRU_TC, LRU_WIDTH), lambda n, c: (n * n_chunk + c, 1)),
            pl.BlockSpec((LRU_TC, LRU_WIDTH), lambda n, c: (n * n_chunk + c, 2)),
            pl.BlockSpec((LRU_CONV, LRU_WIDTH), lambda n, c: (0, 0)),
            row(),
            pl.BlockSpec((LRU_HEADS, LRU_BLOCK, 2 * LRU_BLOCK), lambda n, c: (0, 0, 0)),
            row(), row(), row(),
        ],
        out_specs=[
            pl.BlockSpec((LRU_TC, LRU_WIDTH), lambda n, c: (n * n_chunk + c, 0)),
            pl.BlockSpec((None, 1, LRU_WIDTH), lambda n, c: (n, 0, 0)),
            pl.BlockSpec((None, LRU_CONV - 1, LRU_WIDTH), lambda n, c: (n, 0, 0)),
        ],
        out_shape=[
            jax.ShapeDtypeStruct((m, LRU_WIDTH), F32),
            jax.ShapeDtypeStruct((n_seq, 1, LRU_WIDTH), F32),
            jax.ShapeDtypeStruct((n_seq, LRU_CONV - 1, LRU_WIDTH), F32),
        ],
        scratch_shapes=[
            pltpu.VMEM((LRU_TC + SUBLANES, LRU_WIDTH), F32),
            pltpu.VMEM((LRU_TC, LRU_WIDTH), F32),
            pltpu.VMEM((LRU_TC, LRU_WIDTH), F32),
            pltpu.VMEM((LRU_TC, LRU_WIDTH), F32),
            pltpu.VMEM((1, LRU_WIDTH), F32),
        ],
        compiler_params=_cparams(2),
        name="lru_prompt",
    )(z, z, cw, cb, wax, ba, bx, lam)


def _lru_sample_kernel(xr_ref, yr_ref, buf_ref, h0_ref, cw_ref, cb_ref, wax_ref, ba_ref, bx_ref, lam_ref,
                       o_ref, hl_ref, nb_ref):
    rows = xr_ref.shape[0]
    sb = rows // DEC_SEQ
    x3 = xr_ref[...].reshape(sb, DEC_SEQ, LRU_WIDTH)
    ext = jnp.concatenate([buf_ref[...], x3], axis=1)
    cw = cw_ref[...]
    xc3 = cb_ref[...][None]
    for j in range(LRU_CONV):
        xc3 = xc3 + cw[j:j + 1, :][None] * ext[:, j:j + DEC_SEQ, :]
    xc = xc3.reshape(rows, LRU_WIDTH)

    sp = jax.nn.softplus(-lam_ref[...])
    a, b = _lru_gates(xc, wax_ref, ba_ref[...], bx_ref[...], sp)
    a3 = a.reshape(sb, DEC_SEQ, LRU_WIDTH)
    b3 = b.reshape(sb, DEC_SEQ, LRU_WIDTH)
    h = h0_ref[...]
    hs = []
    for t in range(DEC_SEQ):
        h = a3[:, t:t + 1, :] * h + b3[:, t:t + 1, :]
        hs.append(h)
    hs3 = jnp.concatenate(hs, axis=1)
    o_ref[...] = hs3.reshape(rows, LRU_WIDTH) * jax.nn.gelu(yr_ref[...])
    hl_ref[...] = h
    nb_ref[...] = ext[:, DEC_SEQ:DEC_SEQ + LRU_CONV - 1, :]


def _lru_sample(z, buf, h0, cw, cb, wax, ba, bx, lam):
    m = z.shape[0]
    nb = m // DEC_SEQ
    rows = LRU_SB * DEC_SEQ
    row = lambda: pl.BlockSpec((1, LRU_WIDTH), lambda s: (0, 0))
    return pl.pallas_call(
        _lru_sample_kernel,
        grid=(nb // LRU_SB,),
        in_specs=[
            pl.BlockSpec((rows, LRU_WIDTH), lambda s: (s, 1)),
            pl.BlockSpec((rows, LRU_WIDTH), lambda s: (s, 2)),
            pl.BlockSpec((LRU_SB, LRU_CONV - 1, LRU_WIDTH), lambda s: (s, 0, 0)),
            pl.BlockSpec((LRU_SB, 1, LRU_WIDTH), lambda s: (s, 0, 0)),
            pl.BlockSpec((LRU_CONV, LRU_WIDTH), lambda s: (0, 0)),
            row(),
            pl.BlockSpec((LRU_HEADS, LRU_BLOCK, 2 * LRU_BLOCK), lambda s: (0, 0, 0)),
            row(), row(), row(),
        ],
        out_specs=[
            pl.BlockSpec((rows, LRU_WIDTH), lambda s: (s, 0)),
            pl.BlockSpec((LRU_SB, 1, LRU_WIDTH), lambda s: (s, 0, 0)),
            pl.BlockSpec((LRU_SB, LRU_CONV - 1, LRU_WIDTH), lambda s: (s, 0, 0)),
        ],
        out_shape=[
            jax.ShapeDtypeStruct((m, LRU_WIDTH), F32),
            jax.ShapeDtypeStruct((nb, 1, LRU_WIDTH), F32),
            jax.ShapeDtypeStruct((nb, LRU_CONV - 1, LRU_WIDTH), F32),
        ],
        compiler_params=_cparams(1),
        name="lru_sample",
    )(z, z, buf, h0, cw, cb, wax, ba, bx, lam)


def _ffn_up_kernel(*refs, sample, tiles_per_seq, n_col, n_tiles):
    if sample:
        (h_ref, wg_ref, wv_ref, cwg_ref, cwv_ref, cbg_ref, cbv_ref, stg_ref, stv_ref,
         act_ref, bufg_ref, bufv_ref, ug_scr, uv_scr) = refs
    else:
        (h_ref, wg_ref, wv_ref, cwg_ref, cwv_ref, cbg_ref, cbv_ref,
         act_ref, bufg_ref, bufv_ref, ug_scr, uv_scr, halo_g, halo_v) = refs
    t = pl.program_id(0)
    tm = h_ref.shape[0]
    tf = wg_ref.shape[1]
    pad = SUBLANES
    slot_m = lax.rem(t, 2)
    slot_e = 1 - slot_m

    @pl.when(t == 0)
    def _():
        ug_scr[...] = jnp.zeros(ug_scr.shape, F32)
        uv_scr[...] = jnp.zeros(uv_scr.shape, F32)

    tile = jnp.minimum(t, n_tiles - 1)
    i_m = lax.div(tile, n_col)
    j_m = lax.rem(tile, n_col)
    h = h_ref[...]

    def project(w_ref, u_scr, halo):
        u = jnp.dot(h, w_ref[...], preferred_element_type=F32)
        u_scr[slot_m, pad:, :] = u
        if halo is not None:
            first = lax.rem(i_m, tiles_per_seq) == 0
            u_scr[slot_m, 0:pad, :] = jnp.where(first, 0.0, halo[j_m])
            halo[j_m] = u[tm - pad:, :]

    project(wg_ref, ug_scr, None if sample else halo_g)
    project(wv_ref, uv_scr, None if sample else halo_v)

    def conv_prompt(u_scr, cw_ref, cb_ref, buf_ref):
        u = u_scr[slot_e, pl.ds(pad, tm), :]
        buf_ref[...] = u_scr[slot_e, pl.ds(pad + tm - (FFN_CONV - 1), FFN_CONV - 1), :]
        cw = cw_ref[...]
        out = cb_ref[...]
        for k in range(FFN_CONV - 1):
            out = out + cw[k:k + 1, :] * u_scr[slot_e, pl.ds(pad - (FFN_CONV - 1) + k, tm), :]
        return out + cw[FFN_CONV - 1:FFN_CONV, :] * u

    def conv_sample(u_scr, cw_ref, cb_ref, st_ref, buf_ref):
        u3 = u_scr[slot_e, pl.ds(pad, tm), :].reshape(tm // DEC_SEQ, DEC_SEQ, tf)
        ext = jnp.concatenate([st_ref[...], u3], axis=1)
        buf_ref[...] = u3[:, DEC_SEQ - (FFN_CONV - 1):, :]
        cw = cw_ref[...]
        out = cb_ref[...][None]
        for k in range(FFN_CONV):
            out = out + cw[k:k + 1, :][None] * ext[:, k:k + DEC_SEQ, :]
        return out.reshape(tm, tf)

    if sample:
        gate = conv_sample(ug_scr, cwg_ref, cbg_ref, stg_ref, bufg_ref)
        val = conv_sample(uv_scr, cwv_ref, cbv_ref, stv_ref, bufv_ref)
    else:
        gate = conv_prompt(ug_scr, cwg_ref, cbg_ref, bufg_ref)
        val = conv_prompt(uv_scr, cwv_ref, cbv_ref, bufv_ref)
    act_ref[...] = (jax.nn.silu(gate) * val).astype(BF16)


def _ffn_up(h2, w_up, conv_w, conv_b, state=None):
    m = h2.shape[0]
    sample = state is not None
    tm, tf = (FF_TM_SAMPLE if sample else FF_TM_PROMPT), FF_TF
    n_col = D_FF // tf
    n_row = m // tm
    n_tiles = n_row * n_col

    def mm(t):
        tile = jnp.minimum(t, n_tiles - 1)
        return tile // n_col, tile % n_col

    def ep(t):
        tile = jnp.maximum(t - 1, 0)
        return tile // n_col, tile % n_col

    def col_specs(shape, which, half):
        off = half * n_col
        if len(shape) == 2:
            return pl.BlockSpec(shape, lambda t: (0, which(t)[1] + off))
        return pl.BlockSpec(shape, lambda t: (which(t)[0], 0, which(t)[1] + off))

    in_specs = [pl.BlockSpec((tm, D_MODEL), lambda t: (mm(t)[0], 0), pipeline_mode=pl.Buffered(1)),
                col_specs((D_MODEL, tf), mm, 0), col_specs((D_MODEL, tf), mm, 1),
                col_specs((FFN_CONV, tf), ep, 0), col_specs((FFN_CONV, tf), ep, 1),
                col_specs((1, tf), ep, 0), col_specs((1, tf), ep, 1)]
    args = [h2, w_up, w_up, conv_w, conv_w, conv_b, conv_b]
    scratch = [pltpu.VMEM((2, tm + SUBLANES, tf), F32), pltpu.VMEM((2, tm + SUBLANES, tf), F32)]
    if sample:
        seqs = tm // DEC_SEQ
        n_buf = m // DEC_SEQ
        in_specs += [col_specs((seqs, FFN_CONV - 1, tf), ep, 0), col_specs((seqs, FFN_CONV - 1, tf), ep, 1)]
        args += [state, state]
        buf_spec = lambda: col_specs((seqs, FFN_CONV - 1, tf), ep, 0)
        tiles_per_seq = 1
    else:
        tiles_per_seq = SEQ // tm
        n_buf = n_row
        buf_spec = lambda: col_specs((None, FFN_CONV - 1, tf), ep, 0)
        scratch += [pltpu.VMEM((n_col, SUBLANES, tf), F32), pltpu.VMEM((n_col, SUBLANES, tf), F32)]
    buf_shape = jax.ShapeDtypeStruct((n_buf, FFN_CONV - 1, D_FF), F32)
    return pl.pallas_call(
        functools.partial(_ffn_up_kernel, sample=sample, tiles_per_seq=tiles_per_seq, n_col=n_col, n_tiles=n_tiles),
        grid=(n_tiles + 1,),
        in_specs=in_specs,
        out_specs=[pl.BlockSpec((tm, tf), lambda t: ep(t)), buf_spec(), buf_spec()],
        out_shape=[jax.ShapeDtypeStruct((m, D_FF), BF16), buf_shape, buf_shape],
        scratch_shapes=scratch,
        compiler_params=_cparams(1),
        name="ffn_up_sample" if sample else "ffn_up_prompt",
    )(*args)


def _ffn_down_kernel(act_ref, w_ref, x_ref, g_ref, o_ref, *, k_last_valid):
    k = pl.program_id(1)
    last = pl.num_programs(1) - 1

    @pl.when(k == 0)
    def _():
        o_ref[...] = x_ref[...]

    def accumulate(k_valid):
        act = act_ref[:, :k_valid]
        for c in range(D_MODEL // FF_DOWN_CHUNK):
            cols = slice(c * FF_DOWN_CHUNK, (c + 1) * FF_DOWN_CHUNK)
            o_ref[:, cols] += jnp.dot(act, w_ref[:k_valid, cols], preferred_element_type=F32)

    @pl.when(k < last)
    def _():
        accumulate(FF_TK)

    @pl.when(k == last)
    def _():
        accumulate(k_last_valid)
        for r in range(0, o_ref.shape[0], FF_NORM_ROWS):
            o_ref[r:r + FF_NORM_ROWS, :] = _rms(o_ref[r:r + FF_NORM_ROWS, :], g_ref[...])


def _ffn_down(act, w_down, x2, g_final, name):
    m, k_tot = act.shape
    n_k = pl.cdiv(k_tot, FF_TK)
    return pl.pallas_call(
        functools.partial(_ffn_down_kernel, k_last_valid=k_tot - (n_k - 1) * FF_TK),
        grid=(m // TM, n_k),
        in_specs=[
            pl.BlockSpec((TM, FF_TK), lambda i, k: (i, k)),
            pl.BlockSpec((FF_TK, D_MODEL), lambda i, k: (k, 0)),
            pl.BlockSpec((TM, D_MODEL), lambda i, k: (i, 0)),
            pl.BlockSpec((1, D_MODEL), lambda i, k: (0, 0)),
        ],
        out_specs=pl.BlockSpec((TM, D_MODEL), lambda i, k: (i, 0)),
        out_shape=jax.ShapeDtypeStruct((m, D_MODEL), F32),
        compiler_params=_cparams(2),
        name=name,
    )(act, w_down, x2, g_final)


def _swap_pairs(w):
    shp = w.shape
    return w.reshape(shp[:-1] + (shp[-1] // 2, 2))[..., ::-1].reshape(shp)


def _rope_tables(pos):
    half = QK_ROPE // 2
    freqs = ROPE_THETA ** (-jnp.arange(half, dtype=F32) * 2.0 / QK_ROPE)
    ang = pos.astype(F32)[:, None] * freqs
    cos, sin = jnp.cos(ang), jnp.sin(ang)
    cos_t = jnp.repeat(cos, 2, axis=-1)
    sin_t = jnp.stack([-sin, sin], axis=-1).reshape(pos.shape[0], QK_ROPE)
    return cos_t, sin_t


def kernel(x_prompt, x_sample, cache_ckv, cache_kpe, page_table, state_lru_h, state_lru_conv, state_ffn_conv, g_mix, w_in, g_q, w_uq, g_kv, w_uk, w_uv, lru_conv_w, lru_conv_b, lru_w_a, lru_b_a, lru_w_x, lru_b_x, lru_lambda, g_attn_out, g_lru_out, w_o, g_ffn, w_up, ffn_conv_w, ffn_conv_b, w_down, g_final):
    assert w_in.shape[0] == 1, "single-layer trunk"
    n_p, n_s = x_prompt.shape[0], x_sample.shape[0]
    row = lambda v: v.reshape(1, -1)

    wi = w_in[0]
    c0, c1, c2, c3 = Q_LORA, Q_LORA + KV_LORA, Q_LORA + KV_LORA + QK_ROPE, Q_LORA + KV_LORA + QK_ROPE + LRU_WIDTH
    w_kpe = wi[:, c1:c2]
    zpad = jnp.zeros((D_MODEL, KV_LORA - 2 * QK_ROPE), F32)
    w_in_p = jnp.concatenate([wi[:, :c1], w_kpe, _swap_pairs(w_kpe), zpad, wi[:, c2:c3], wi[:, c3:]],
                             axis=1).astype(BF16)
    wq = w_uq[0]
    wq_rope = wq[:, :, QK_NOPE:]
    w_q_p = jnp.concatenate([wq[:, :, :QK_NOPE].reshape(Q_LORA, -1), wq_rope.reshape(Q_LORA, -1),
                             _swap_pairs(wq_rope).reshape(Q_LORA, -1)], axis=1).astype(BF16)
    w_kv = jnp.concatenate([w_uk[0].reshape(KV_LORA, -1), w_uv[0].reshape(KV_LORA, -1)], axis=1).astype(BF16)
    w_ukt = jnp.transpose(w_uk[0], (1, 2, 0)).astype(BF16)
    w_uvh = jnp.transpose(w_uv[0], (1, 0, 2)).astype(BF16)
    wax = jnp.concatenate([lru_w_a[0], lru_w_x[0]], axis=-1).astype(BF16)
    w_o_b = w_o[0].astype(BF16)
    w_up_b = w_up[0].astype(BF16)
    w_down_b = w_down[0].astype(BF16)

    cos_p, sin_p = _rope_tables(jnp.arange(SEQ))
    cos_s, sin_s = _rope_tables(PAST_LEN + jnp.arange(DEC_SEQ))
    cos_s, sin_s = jnp.tile(cos_s, (n_s, 1)), jnp.tile(sin_s, (n_s, 1))

    lru_args = (lru_conv_w[0], row(lru_conv_b[0]), wax, row(lru_b_a[0]), row(lru_b_x[0]), row(lru_lambda[0]))

    def trunk(x, cos_k, sin_k, tag):
        z = _norm_matmul([(x, 0, D_MODEL, row(g_mix[0]))], w_in_p, None, IN_TN, F32, "in_proj_" + tag)
        qn, qp = _q_proj(z, row(g_q[0]), w_q_p, jnp.tile(cos_k, (1, MLA_HEADS)), jnp.tile(sin_k, (1, MLA_HEADS)),
                         "q_proj_" + tag)
        ckv, kpe = _kv_post(z, row(g_kv[0]), cos_k, sin_k, "kv_post_" + tag)
        return z, qn, qp, ckv, kpe

    def tail(x, o_attn, o_lru, tag, state=None):
        x2 = _norm_matmul([(o_attn, 0, ATTN_WIDTH, row(g_attn_out[0])), (o_lru, 0, LRU_WIDTH, row(g_lru_out[0]))],
                          w_o_b, x, IN_TN, F32, "out_proj_" + tag)
        h2 = _norm_cast(x2, row(g_ffn[0]), "ffn_norm_" + tag)
        act, buf_g, buf_v = _ffn_up(h2, w_up_b, ffn_conv_w[0], row(ffn_conv_b[0]), state)
        y = _ffn_down(act, w_down_b, x2, row(g_final), "ffn_down_" + tag)
        if state is None:
            per_seq = SEQ // FF_TM_PROMPT
            buf_g, buf_v = buf_g[per_seq - 1::per_seq], buf_v[per_seq - 1::per_seq]
        new_buf = jnp.concatenate([buf_g, buf_v], axis=-1)
        return y, new_buf

    xp = x_prompt.reshape(n_p * SEQ, D_MODEL)
    z_p, qn_p, qp_p, ckv_p, kpe_p = trunk(xp, cos_p, sin_p, "p")
    kn_p, v_p = _kv_up(ckv_p, w_kv)
    o_attn_p = _prompt_attn(qn_p, qp_p, kn_p, kpe_p, v_p)
    o_lru_p, hl_p, lbuf_p = _lru_prompt(z_p, *lru_args)
    y_p, fbuf_p = tail(xp, o_attn_p, o_lru_p, "p")

    xs = x_sample.reshape(n_s * DEC_SEQ, D_MODEL)
    z_s, qn_s, qp_s, ckv_s, kpe_s = trunk(xs, cos_s, sin_s, "s")
    q_lat = _q_lat(qn_s, w_ukt)
    q_pe = qp_s.reshape(MLA_HEADS, n_s, DEC_SEQ, QK_ROPE).transpose(1, 0, 2, 3).reshape(
        n_s, MLA_HEADS * DEC_SEQ, QK_ROPE)
    o_lat = _paged_attn(page_table, q_lat, q_pe, ckv_s, kpe_s, cache_ckv, jnp.swapaxes(cache_kpe, 2, 3))
    o_attn_s = _o_up(o_lat, w_uvh)
    o_lru_s, hl_s, lbuf_s = _lru_sample(z_s, state_lru_conv[0], state_lru_h[0].reshape(n_s, 1, LRU_WIDTH), *lru_args)
    y_s, fbuf_s = tail(xs, o_attn_s, o_lru_s, "s", state_ffn_conv[0])

    return (
        y_p.reshape(n_p, SEQ, D_MODEL),
        y_s.reshape(n_s, DEC_SEQ, D_MODEL),
        ckv_p.reshape(1, n_p, SEQ, KV_LORA),
        kpe_p.reshape(1, n_p, SEQ, QK_ROPE),
        ckv_s.reshape(1, n_s, DEC_SEQ, KV_LORA),
        kpe_s.reshape(1, n_s, DEC_SEQ, QK_ROPE),
        hl_p.reshape(1, n_p, LRU_WIDTH),
        hl_s.reshape(1, n_s, LRU_WIDTH),
        lbuf_p[None],
        lbuf_s[None],
        fbuf_p[None],
        fbuf_s[None],
    )
```

```python
import functools

import jax
import jax.numpy as jnp
from jax import lax
from jax.experimental import pallas as pl
from jax.experimental.pallas import tpu as pltpu

F32 = jnp.float32
BF16 = jnp.bfloat16

D_MODEL = 4096
SEQ = 2048
DEC_SEQ = 8
PAST_LEN = 16384
PAGE_SIZE = 128
MLA_HEADS = 16
QK_NOPE = 128
QK_ROPE = 64
V_HEAD = 128
Q_LORA = 1024
KV_LORA = 512
ROPE_THETA = 10000.0
SM_SCALE = (QK_NOPE + QK_ROPE) ** -0.5
LRU_WIDTH = D_MODEL // 2
LRU_HEADS = 16
LRU_BLOCK = LRU_WIDTH // LRU_HEADS
LRU_CONV = 4
LRU_C = 8.0
D_FF = 11008
FFN_CONV = 3
EPS = 1e-6
ATTN_WIDTH = MLA_HEADS * V_HEAD

VMEM_LIMIT_BYTES = 56 * 1024 * 1024
SUBLANES = 8

TM = 512
IN_PAD = 6144
IN_TN = 1024
FF_TM_PROMPT = 2048
FF_TM_SAMPLE = 1024
FF_TF = 256
FF_TK = 1024
FF_DOWN_CHUNK = 1024
FF_NORM_ROWS = 128
LRU_TC = 256
LRU_SB = 32
ATT_QB = 512
PAGE_GROUP = 16
PAGE_SLOTS = 4
ATT_CHAINS = 4


def _cparams(n_axes):
    return pltpu.CompilerParams(dimension_semantics=("arbitrary",) * n_axes,
                                vmem_limit_bytes=VMEM_LIMIT_BYTES)


def _rms(x, g):
    return x * lax.rsqrt(jnp.mean(x * x, axis=-1, keepdims=True) + EPS) * g


def _norm_matmul_kernel(*refs, n_a, has_res):
    a_refs = refs[0:2 * n_a:2]
    g_refs = refs[1:2 * n_a:2]
    w_ref = refs[2 * n_a]
    res_ref = refs[2 * n_a + 1] if has_res else None
    o_ref, h_scr = refs[-2], refs[-1]

    @pl.when(pl.program_id(1) == 0)
    def _():
        off = 0
        for a_ref, g_ref in zip(a_refs, g_refs):
            k = a_ref.shape[-1]
            h_scr[:, off:off + k] = _rms(a_ref[...], g_ref[...]).astype(BF16)
            off += k

    acc = jnp.dot(h_scr[...], w_ref[...], preferred_element_type=F32)
    if has_res:
        acc = acc + res_ref[...]
    o_ref[...] = acc.astype(o_ref.dtype)


def _norm_matmul(a_list, w, res, tn, out_dtype, name):
    m = a_list[0][0].shape[0]
    k_tot, n = w.shape
    in_specs, args = [], []
    for arr, cb, ka, g in a_list:
        in_specs.append(pl.BlockSpec((TM, ka), lambda i, j, cb=cb: (i, cb)))
        in_specs.append(pl.BlockSpec((1, ka), lambda i, j: (0, 0)))
        args += [arr, g]
    in_specs.append(pl.BlockSpec((k_tot, tn), lambda i, j: (0, j)))
    args.append(w)
    if res is not None:
        in_specs.append(pl.BlockSpec((TM, tn), lambda i, j: (i, j)))
        args.append(res)
    return pl.pallas_call(
        functools.partial(_norm_matmul_kernel, n_a=len(a_list), has_res=res is not None),
        grid=(m // TM, n // tn),
        in_specs=in_specs,
        out_specs=pl.BlockSpec((TM, tn), lambda i, j: (i, j)),
        out_shape=jax.ShapeDtypeStruct((m, n), out_dtype),
        scratch_shapes=[pltpu.VMEM((TM, k_tot), BF16)],
        compiler_params=_cparams(2),
        name=name,
    )(*args)


def _norm_cast_kernel(x_ref, g_ref, o_ref):
    o_ref[...] = _rms(x_ref[...], g_ref[...]).astype(BF16)


def _norm_cast(x, g, name):
    m, k = x.shape
    return pl.pallas_call(
        _norm_cast_kernel,
        grid=(m // TM,),
        in_specs=[pl.BlockSpec((TM, k), lambda i: (i, 0)), pl.BlockSpec((1, k), lambda i: (0, 0))],
        out_specs=pl.BlockSpec((TM, k), lambda i: (i, 0)),
        out_shape=jax.ShapeDtypeStruct((m, k), BF16),
        compiler_params=_cparams(1),
        name=name,
    )(x, g)


def _q_proj_kernel(a_ref, g_ref, w_ref, cos_ref, sin_ref, qn_ref, qp_ref, h_scr):
    j = pl.program_id(1)

    @pl.when(j == 0)
    def _():
        h_scr[...] = _rms(a_ref[...], g_ref[...]).astype(BF16)

    acc = jnp.dot(h_scr[...], w_ref[...], preferred_element_type=F32)

    @pl.when(j == 0)
    def _():
        for h in range(MLA_HEADS):
            qn_ref[h] = acc[:, h * QK_NOPE:(h + 1) * QK_NOPE].astype(BF16)

    @pl.when(j == 1)
    def _():
        half = MLA_HEADS * QK_ROPE
        roped = acc[:, :half] * cos_ref[...] + acc[:, half:] * sin_ref[...]
        for h in range(MLA_HEADS):
            qp_ref[h] = roped[:, h * QK_ROPE:(h + 1) * QK_ROPE].astype(BF16)


def _q_proj(z, g_q, w_q, cos_q, sin_q, name):
    m = z.shape[0]
    n_tab = cos_q.shape[0] // TM
    half = MLA_HEADS * QK_ROPE
    return pl.pallas_call(
        _q_proj_kernel,
        grid=(m // TM, 2),
        in_specs=[
            pl.BlockSpec((TM, Q_LORA), lambda i, j: (i, 0)),
            pl.BlockSpec((1, Q_LORA), lambda i, j: (0, 0)),
            pl.BlockSpec((Q_LORA, 2 * half), lambda i, j: (0, j)),
            pl.BlockSpec((TM, half), lambda i, j: (i % n_tab, 0)),
            pl.BlockSpec((TM, half), lambda i, j: (i % n_tab, 0)),
        ],
        out_specs=[
            pl.BlockSpec((MLA_HEADS, TM, QK_NOPE), lambda i, j: (0, i, 0)),
            pl.BlockSpec((MLA_HEADS, TM, QK_ROPE), lambda i, j: (0, i, 0)),
        ],
        out_shape=[
            jax.ShapeDtypeStruct((MLA_HEADS, m, QK_NOPE), BF16),
            jax.ShapeDtypeStruct((MLA_HEADS, m, QK_ROPE), BF16),
        ],
        scratch_shapes=[pltpu.VMEM((TM, Q_LORA), BF16)],
        compiler_params=_cparams(2),
        name=name,
    )(z, g_q, w_q, cos_q, sin_q)


def _kv_post_kernel(zkv_ref, zpe_ref, g_ref, cos_ref, sin_ref, ckv_ref, kpe_ref):
    ckv_ref[...] = _rms(zkv_ref[...], g_ref[...])
    zpe = zpe_ref[...]
    kpe_ref[...] = zpe[:, :QK_ROPE] * cos_ref[...] + zpe[:, QK_ROPE:2 * QK_ROPE] * sin_ref[...]


def _kv_post(z, g_kv, cos_k, sin_k, name):
    m = z.shape[0]
    n_tab = cos_k.shape[0] // TM
    return pl.pallas_call(
        _kv_post_kernel,
        grid=(m // TM,),
        in_specs=[
            pl.BlockSpec((TM, KV_LORA), lambda i: (i, Q_LORA // KV_LORA)),
            pl.BlockSpec((TM, KV_LORA), lambda i: (i, Q_LORA // KV_LORA + 1)),
            pl.BlockSpec((1, KV_LORA), lambda i: (0, 0)),
            pl.BlockSpec((TM, QK_ROPE), lambda i: (i % n_tab, 0)),
            pl.BlockSpec((TM, QK_ROPE), lambda i: (i % n_tab, 0)),
        ],
        out_specs=[
            pl.BlockSpec((TM, KV_LORA), lambda i: (i, 0)),
            pl.BlockSpec((TM, QK_ROPE), lambda i: (i, 0)),
        ],
        out_shape=[
            jax.ShapeDtypeStruct((m, KV_LORA), F32),
            jax.ShapeDtypeStruct((m, QK_ROPE), F32),
        ],
        compiler_params=_cparams(1),
        name=name,
    )(z, z, g_kv, cos_k, sin_k)


def _kv_up_kernel(c_ref, w_ref, kn_ref, v_ref):
    acc = jnp.dot(c_ref[...].astype(BF16), w_ref[...], preferred_element_type=F32)
    j = pl.program_id(1)

    @pl.when(j == 0)
    def _():
        for h in range(MLA_HEADS):
            kn_ref[h] = acc[:, h * QK_NOPE:(h + 1) * QK_NOPE].astype(BF16)

    @pl.when(j == 1)
    def _():
        for h in range(MLA_HEADS):
            v_ref[h] = acc[:, h * V_HEAD:(h + 1) * V_HEAD].astype(BF16)


def _kv_up(ckv, w_kv):
    m = ckv.shape[0]
    return pl.pallas_call(
        _kv_up_kernel,
        grid=(m // TM, 2),
        in_specs=[
            pl.BlockSpec((TM, KV_LORA), lambda i, j: (i, 0)),
            pl.BlockSpec((KV_LORA, MLA_HEADS * QK_NOPE), lambda i, j: (0, j)),
        ],
        out_specs=[
            pl.BlockSpec((MLA_HEADS, TM, QK_NOPE), lambda i, j: (0, i, 0)),
            pl.BlockSpec((MLA_HEADS, TM, V_HEAD), lambda i, j: (0, i, 0)),
        ],
        out_shape=[
            jax.ShapeDtypeStruct((MLA_HEADS, m, QK_NOPE), BF16),
            jax.ShapeDtypeStruct((MLA_HEADS, m, V_HEAD), BF16),
        ],
        compiler_params=_cparams(2),
        name="kv_up",
    )(ckv, w_kv)


def _prompt_attn_kernel(qn_ref, qp_ref, kn_ref, kpe_ref, v_ref, o_ref):
    k_cat = jnp.concatenate([kn_ref[...], kpe_ref[...].astype(BF16)], axis=-1)
    for b in range(SEQ // ATT_QB):
        rows = slice(b * ATT_QB, (b + 1) * ATT_QB)
        n_keys = (b + 1) * ATT_QB
        q = jnp.concatenate([qn_ref[rows, :], qp_ref[rows, :]], axis=-1)
        s = lax.dot_general(q, k_cat[:n_keys], (((1,), (1,)), ((), ())),
                            preferred_element_type=F32) * SM_SCALE
        q_pos = b * ATT_QB + lax.broadcasted_iota(jnp.int32, s.shape, 0)
        k_pos = lax.broadcasted_iota(jnp.int32, s.shape, 1)
        s = jnp.where(k_pos <= q_pos, s, -jnp.inf)
        p = jnp.exp(s - jnp.max(s, axis=-1, keepdims=True))
        l = jnp.sum(p, axis=-1, keepdims=True)
        o = jnp.dot(p.astype(BF16), v_ref[:n_keys, :], preferred_element_type=F32)
        o_ref[rows, :] = o / l


def _prompt_attn(qn, qp, kn, kpe, v):
    n_seq = qn.shape[1] // SEQ
    return pl.pallas_call(
        _prompt_attn_kernel,
        grid=(n_seq, MLA_HEADS),
        in_specs=[
            pl.BlockSpec((None, SEQ, QK_NOPE), lambda n, h: (h, n, 0)),
            pl.BlockSpec((None, SEQ, QK_ROPE), lambda n, h: (h, n, 0)),
            pl.BlockSpec((None, SEQ, QK_NOPE), lambda n, h: (h, n, 0)),
            pl.BlockSpec((SEQ, QK_ROPE), lambda n, h: (n, 0)),
            pl.BlockSpec((None, SEQ, V_HEAD), lambda n, h: (h, n, 0)),
        ],
        out_specs=pl.BlockSpec((SEQ, V_HEAD), lambda n, h: (n, h)),
        out_shape=jax.ShapeDtypeStruct((n_seq * SEQ, ATTN_WIDTH), F32),
        compiler_params=_cparams(2),
        name="prompt_attn",
    )(qn, qp, kn, kpe, v)


def _q_lat_kernel(qn_ref, w_ref, o_ref):
    acc = jnp.dot(qn_ref[...], w_ref[...], preferred_element_type=F32)
    o_ref[...] = acc.reshape(o_ref.shape)


def _q_lat(qn_s, w_ukt):
    ms = qn_s.shape[1]
    nb = ms // DEC_SEQ
    return pl.pallas_call(
        _q_lat_kernel,
        grid=(MLA_HEADS,),
        in_specs=[
            pl.BlockSpec((None, ms, QK_NOPE), lambda h: (h, 0, 0)),
            pl.BlockSpec((None, QK_NOPE, KV_LORA), lambda h: (h, 0, 0)),
        ],
        out_specs=pl.BlockSpec((nb, DEC_SEQ, KV_LORA), lambda h: (0, h, 0)),
        out_shape=jax.ShapeDtypeStruct((nb, MLA_HEADS * DEC_SEQ, KV_LORA), F32),
        compiler_params=_cparams(1),
        name="q_lat",
    )(qn_s, w_ukt)


def _paged_attn_kernel(pt_ref, qlat_ref, qpe_ref, cnew_ref, knew_ref, *rest):
    ckv_hbm, kpet_hbm, o_ref, cbuf, pbuf, sems, ql_scr, kb_scr, m_scr, l_scr, acc_scr = rest
    n = pl.program_id(0)
    n_groups = pt_ref.shape[1] // PAGE_GROUP
    total = pl.num_programs(0) * n_groups
    per = PAGE_GROUP // ATT_CHAINS
    nt = (((1,), (1,)), ((), ()))

    def group_copies(t):
        seq = lax.div(t, n_groups)
        first = lax.rem(t, n_groups) * PAGE_GROUP
        slot = lax.rem(t, PAGE_SLOTS)
        copies = []
        for k in range(PAGE_GROUP):
            page = pt_ref[seq, first + k]
            copies.append(pltpu.make_async_copy(ckv_hbm.at[0, page], cbuf.at[slot, k], sems.at[0, slot]))
            copies.append(pltpu.make_async_copy(kpet_hbm.at[0, page], pbuf.at[slot, k], sems.at[1, slot]))
        return copies

    def start_group(t):
        for idx, cp in enumerate(group_copies(t)):
            cp.start(priority=(idx // 2) % 2)

    def wait_group(t):
        for cp in group_copies(t):
            cp.wait()

    def online(c, s, values):
        m_old = m_scr[c]
        m_new = jnp.maximum(m_old, jnp.max(s, axis=-1, keepdims=True))
        alpha = jnp.exp(m_old - m_new)
        p = jnp.exp(s - m_new)
        l_scr[c] = alpha * l_scr[c] + jnp.sum(p, axis=-1, keepdims=True)
        acc_scr[c] = alpha * acc_scr[c] + jnp.dot(p.astype(BF16), values, preferred_element_type=F32)
        m_scr[c] = m_new

    @pl.when(n == 0)
    def _():
        for t0 in range(PAGE_SLOTS - 1):
            start_group(jnp.int32(t0))

    ql_scr[...] = qlat_ref[...].astype(BF16)
    m_scr[...] = jnp.full(m_scr.shape, -jnp.inf, F32)
    l_scr[...] = jnp.zeros(l_scr.shape, F32)
    acc_scr[...] = jnp.zeros(acc_scr.shape, F32)
    c_new = cnew_ref[...].astype(BF16)
    s_new = (lax.dot_general(ql_scr[...], c_new, nt, preferred_element_type=F32)
             + lax.dot_general(qpe_ref[...], knew_ref[...].astype(BF16), nt, preferred_element_type=F32)) * SM_SCALE
    q_pos = lax.broadcasted_iota(jnp.int32, s_new.shape, 0) % DEC_SEQ
    k_pos = lax.broadcasted_iota(jnp.int32, s_new.shape, 1)
    online(0, jnp.where(k_pos <= q_pos, s_new, -jnp.inf), c_new)

    def group_step(g, carry):
        t = n * n_groups + g
        slot = lax.rem(t, PAGE_SLOTS)
        wait_group(t)

        @pl.when(t + (PAGE_SLOTS - 1) < total)
        def _():
            start_group(t + (PAGE_SLOTS - 1))

        for k in range(PAGE_GROUP):
            kb_scr[k * PAGE_SIZE:(k + 1) * PAGE_SIZE, :] = cbuf[slot, k].astype(BF16)
        scores = []
        for c in range(ATT_CHAINS):
            lo = c * per * PAGE_SIZE
            kpt = jnp.concatenate([pbuf[slot, c * per + k].astype(BF16) for k in range(per)], axis=-1)
            scores.append((lax.dot_general(ql_scr[...], kb_scr[lo:lo + per * PAGE_SIZE, :], nt,
                                           preferred_element_type=F32)
                           + jnp.dot(qpe_ref[...], kpt, preferred_element_type=F32)) * SM_SCALE)
        probs, alphas = [], []
        for c in range(ATT_CHAINS):
            m_old = m_scr[c]
            m_new = jnp.maximum(m_old, jnp.max(scores[c], axis=-1, keepdims=True))
            alpha = jnp.exp(m_old - m_new)
            p = jnp.exp(scores[c] - m_new)
            l_scr[c] = alpha * l_scr[c] + jnp.sum(p, axis=-1, keepdims=True)
            m_scr[c] = m_new
            probs.append(p.astype(BF16))
            alphas.append(alpha)
        for c in range(ATT_CHAINS):
            lo = c * per * PAGE_SIZE
            acc_scr[c] = alphas[c] * acc_scr[c] + jnp.dot(probs[c], kb_scr[lo:lo + per * PAGE_SIZE, :],
                                                          preferred_element_type=F32)
        return carry

    lax.fori_loop(0, n_groups, group_step, 0)

    m = m_scr[0]
    for c in range(1, ATT_CHAINS):
        m = jnp.maximum(m, m_scr[c])
    l = jnp.zeros(m.shape, F32)
    acc = jnp.zeros(acc_scr.shape[1:], F32)
    for c in range(ATT_CHAINS):
        w = jnp.exp(m_scr[c] - m)
        l = l + w * l_scr[c]
        acc = acc + w * acc_scr[c]
    o_ref[...] = acc / l


def _paged_attn(page_table, q_lat, q_pe, ckv_new, kpe_new, cache_ckv, cache_kpe_t):
    nb = q_lat.shape[0]
    rows = MLA_HEADS * DEC_SEQ
    n_pages = page_table.shape[1]
    assert n_pages % PAGE_GROUP == 0 and nb * (n_pages // PAGE_GROUP) >= PAGE_SLOTS - 1

    grid_spec = pltpu.PrefetchScalarGridSpec(
        num_scalar_prefetch=1,
        grid=(nb,),
        in_specs=[
            pl.BlockSpec((None, rows, KV_LORA), lambda n, pt: (n, 0, 0)),
            pl.BlockSpec((None, rows, QK_ROPE), lambda n, pt: (n, 0, 0)),
            pl.BlockSpec((DEC_SEQ, KV_LORA), lambda n, pt: (n, 0)),
            pl.BlockSpec((DEC_SEQ, QK_ROPE), lambda n, pt: (n, 0)),
            pl.BlockSpec(memory_space=pl.ANY),
            pl.BlockSpec(memory_space=pl.ANY),
        ],
        out_specs=pl.BlockSpec((None, rows, KV_LORA), lambda n, pt: (n, 0, 0)),
        scratch_shapes=[
            pltpu.VMEM((PAGE_SLOTS, PAGE_GROUP, PAGE_SIZE, KV_LORA), F32),
            pltpu.VMEM((PAGE_SLOTS, PAGE_GROUP, QK_ROPE, PAGE_SIZE), F32),
            pltpu.SemaphoreType.DMA((2, PAGE_SLOTS)),
            pltpu.VMEM((rows, KV_LORA), BF16),
            pltpu.VMEM((PAGE_GROUP * PAGE_SIZE, KV_LORA), BF16),
            pltpu.VMEM((ATT_CHAINS, rows, 1), F32),
            pltpu.VMEM((ATT_CHAINS, rows, 1), F32),
            pltpu.VMEM((ATT_CHAINS, rows, KV_LORA), F32),
        ],
    )
    return pl.pallas_call(
        _paged_attn_kernel,
        grid_spec=grid_spec,
        out_shape=jax.ShapeDtypeStruct((nb, rows, KV_LORA), F32),
        compiler_params=_cparams(1),
        name="paged_attn",
    )(page_table, q_lat, q_pe, ckv_new, kpe_new, cache_ckv, cache_kpe_t)


def _o_up_kernel(o_ref, w_ref, out_ref):
    o = o_ref[...]
    o = o.reshape(o.shape[0] * o.shape[1], o.shape[2]).astype(BF16)
    out_ref[...] = jnp.dot(o, w_ref[...], preferred_element_type=F32)


def _o_up(o_lat, w_uvh):
    nb = o_lat.shape[0]
    ms = nb * DEC_SEQ
    return pl.pallas_call(
        _o_up_kernel,
        grid=(MLA_HEADS,),
        in_specs=[
            pl.BlockSpec((nb, DEC_SEQ, KV_LORA), lambda h: (0, h, 0)),
            pl.BlockSpec((None, KV_LORA, V_HEAD), lambda h: (h, 0, 0)),
        ],
        out_specs=pl.BlockSpec((ms, V_HEAD), lambda h: (0, h)),
        out_shape=jax.ShapeDtypeStruct((ms, ATTN_WIDTH), F32),
        compiler_params=_cparams(1),
        name="o_up",
    )(o_lat, w_uvh)


def _lru_gates(xc, wax_ref, ba, bx, sp):
    a_parts, b_parts = [], []
    for blk in range(LRU_HEADS):
        sl = slice(blk * LRU_BLOCK, (blk + 1) * LRU_BLOCK)
        xb = xc[:, sl]
        y = jnp.dot(xb.astype(BF16), wax_ref[blk], preferred_element_type=F32)
        r = jax.nn.sigmoid(y[:, :LRU_BLOCK] + ba[:, sl])
        i_gate = jax.nn.sigmoid(y[:, LRU_BLOCK:] + bx[:, sl])
        a = jnp.exp(-LRU_C * r * sp[:, sl])
        a_parts.append(a)
        b_parts.append(jnp.sqrt(1.0 - a * a) * (i_gate * xb))
    return jnp.concatenate(a_parts, axis=-1), jnp.concatenate(b_parts, axis=-1)


def _lru_prompt_kernel(xr_ref, yr_ref, cw_ref, cb_ref, wax_ref, ba_ref, bx_ref, lam_ref,
                       o_ref, hl_ref, nb_ref, xext, a_scr, b_scr, hs_scr, h_carry):
    c = pl.program_id(1)
    tc = xr_ref.shape[0]
    pad = SUBLANES

    @pl.when(c == 0)
    def _():
        xext[0:pad, :] = jnp.zeros((pad, LRU_WIDTH), F32)
        h_carry[...] = jnp.zeros(h_carry.shape, F32)

    @pl.when(c > 0)
    def _():
        xext[0:pad, :] = xext[tc:tc + pad, :]

    xr = xr_ref[...]
    xext[pad:, :] = xr
    cw = cw_ref[...]
    xc = cb_ref[...]
    for j in range(LRU_CONV - 1):
        xc = xc + cw[j:j + 1, :] * xext[pl.ds(pad - (LRU_CONV - 1) + j, tc), :]
    xc = xc + cw[LRU_CONV - 1:LRU_CONV, :] * xr

    sp = jax.nn.softplus(-lam_ref[...])
    a, b = _lru_gates(xc, wax_ref, ba_ref[...], bx_ref[...], sp)
    a_scr[...] = a
    b_scr[...] = b

    def step(t, h):
        h = a_scr[pl.ds(t, 1), :] * h + b_scr[pl.ds(t, 1), :]
        hs_scr[pl.ds(t, 1), :] = h
        return h

    h = lax.fori_loop(0, tc, step, h_carry[...], unroll=8)
    h_carry[...] = h
    o_ref[...] = hs_scr[...] * jax.nn.gelu(yr_ref[...])
    hl_ref[...] = h
    nb_ref[...] = xext[tc + pad - (LRU_CONV - 1):tc + pad, :]


def _lru_prompt(z, cw, cb, wax, ba, bx, lam):
    m = z.shape[0]
    n_seq = m // SEQ
    n_chunk = SEQ // LRU_TC
    row = lambda: pl.BlockSpec((1, LRU_WIDTH), lambda n, c: (0, 0))
    return pl.pallas_call(
        _lru_prompt_kernel,
        grid=(n_seq, n_chunk),
        in_specs=[
            pl.BlockSpec((LRU_TC, LRU_WIDTH), lambda n, c: (n * n_chunk + c, 1)),
            pl.BlockSpec((LRU_TC, LRU_WIDTH), lambda n, c: (n * n_chunk + c, 2)),
            pl.BlockSpec((LRU_CONV, LRU_WIDTH), lambda n, c: (0, 0)),
            row(),
            pl.BlockSpec((LRU_HEADS, LRU_BLOCK, 2 * LRU_BLOCK), lambda n, c: (0, 0, 0)),
            row(), row(), row(),
        ],
        out_specs=[
            pl.BlockSpec((LRU_TC, LRU_WIDTH), lambda n, c: (n * n_chunk + c, 0)),
            pl.BlockSpec((None, 1, LRU_WIDTH), lambda n, c: (n, 0, 0)),
            pl.BlockSpec((None, LRU_CONV - 1, LRU_WIDTH), lambda n, c: (n, 0, 0)),
        ],
        out_shape=[
            jax.ShapeDtypeStruct((m, LRU_WIDTH), F32),
            jax.ShapeDtypeStruct((n_seq, 1, LRU_WIDTH), F32),
            jax.ShapeDtypeStruct((n_seq, LRU_CONV - 1, LRU_WIDTH), F32),
        ],
        scratch_shapes=[
            pltpu.VMEM((LRU_TC + SUBLANES, LRU_WIDTH), F32),
            pltpu.VMEM((LRU_TC, LRU_WIDTH), F32),
            pltpu.VMEM((LRU_TC, LRU_WIDTH), F32),
            pltpu.VMEM((LRU_TC, LRU_WIDTH), F32),
            pltpu.VMEM((1, LRU_WIDTH), F32),
        ],
        compiler_params=_cparams(2),
        name="lru_prompt",
    )(z, z, cw, cb, wax, ba, bx, lam)


def _lru_sample_kernel(xr_ref, yr_ref, buf_ref, h0_ref, cw_ref, cb_ref, wax_ref, ba_ref, bx_ref, lam_ref,
                       o_ref, hl_ref, nb_ref):
    rows = xr_ref.shape[0]
    sb = rows // DEC_SEQ
    x3 = xr_ref[...].reshape(sb, DEC_SEQ, LRU_WIDTH)
    ext = jnp.concatenate([buf_ref[...], x3], axis=1)
    cw = cw_ref[...]
    xc3 = cb_ref[...][None]
    for j in range(LRU_CONV):
        xc3 = xc3 + cw[j:j + 1, :][None] * ext[:, j:j + DEC_SEQ, :]
    xc = xc3.reshape(rows, LRU_WIDTH)

    sp = jax.nn.softplus(-lam_ref[...])
    a, b = _lru_gates(xc, wax_ref, ba_ref[...], bx_ref[...], sp)
    a3 = a.reshape(sb, DEC_SEQ, LRU_WIDTH)
    b3 = b.reshape(sb, DEC_SEQ, LRU_WIDTH)
    h = h0_ref[...]
    hs = []
    for t in range(DEC_SEQ):
        h = a3[:, t:t + 1, :] * h + b3[:, t:t + 1, :]
        hs.append(h)
    hs3 = jnp.concatenate(hs, axis=1)
    o_ref[...] = hs3.reshape(rows, LRU_WIDTH) * jax.nn.gelu(yr_ref[...])
    hl_ref[...] = h
    nb_ref[...] = ext[:, DEC_SEQ:DEC_SEQ + LRU_CONV - 1, :]


def _lru_sample(z, buf, h0, cw, cb, wax, ba, bx, lam):
    m = z.shape[0]
    nb = m // DEC_SEQ
    rows = LRU_SB * DEC_SEQ
    row = lambda: pl.BlockSpec((1, LRU_WIDTH), lambda s: (0, 0))
    return pl.pallas_call(
        _lru_sample_kernel,
        grid=(nb // LRU_SB,),
        in_specs=[
            pl.BlockSpec((rows, LRU_WIDTH), lambda s: (s, 1)),
            pl.BlockSpec((rows, LRU_WIDTH), lambda s: (s, 2)),
            pl.BlockSpec((LRU_SB, LRU_CONV - 1, LRU_WIDTH), lambda s: (s, 0, 0)),
            pl.BlockSpec((LRU_SB, 1, LRU_WIDTH), lambda s: (s, 0, 0)),
            pl.BlockSpec((LRU_CONV, LRU_WIDTH), lambda s: (0, 0)),
            row(),
            pl.BlockSpec((LRU_HEADS, LRU_BLOCK, 2 * LRU_BLOCK), lambda s: (0, 0, 0)),
            row(), row(), row(),
        ],
        out_specs=[
            pl.BlockSpec((rows, LRU_WIDTH), lambda s: (s, 0)),
            pl.BlockSpec((LRU_SB, 1, LRU_WIDTH), lambda s: (s, 0, 0)),
            pl.BlockSpec((LRU_SB, LRU_CONV - 1, LRU_WIDTH), lambda s: (s, 0, 0)),
        ],
        out_shape=[
            jax.ShapeDtypeStruct((m, LRU_WIDTH), F32),
            jax.ShapeDtypeStruct((nb, 1, LRU_WIDTH), F32),
            jax.ShapeDtypeStruct((nb, LRU_CONV - 1, LRU_WIDTH), F32),
        ],
        compiler_params=_cparams(1),
        name="lru_sample",
    )(z, z, buf, h0, cw, cb, wax, ba, bx, lam)


def _ffn_up_kernel(*refs, sample, tiles_per_seq, n_col, n_tiles):
    if sample:
        (h_ref, wg_ref, wv_ref, cwg_ref, cwv_ref, cbg_ref, cbv_ref, stg_ref, stv_ref,
         act_ref, bufg_ref, bufv_ref, ug_scr, uv_scr) = refs
    else:
        (h_ref, wg_ref, wv_ref, cwg_ref, cwv_ref, cbg_ref, cbv_ref,
         act_ref, bufg_ref, bufv_ref, ug_scr, uv_scr, halo_g, halo_v) = refs
    t = pl.program_id(0)
    tm = h_ref.shape[0]
    tf = wg_ref.shape[1]
    pad = SUBLANES
    slot_m = lax.rem(t, 2)
    slot_e = 1 - slot_m

    @pl.when(t == 0)
    def _():
        ug_scr[...] = jnp.zeros(ug_scr.shape, F32)
        uv_scr[...] = jnp.zeros(uv_scr.shape, F32)

    tile = jnp.minimum(t, n_tiles - 1)
    i_m = lax.div(tile, n_col)
    j_m = lax.rem(tile, n_col)
    h = h_ref[...]

    def project(w_ref, u_scr, halo):
        u = jnp.dot(h, w_ref[...], preferred_element_type=F32)
        u_scr[slot_m, pad:, :] = u
        if halo is not None:
            first = lax.rem(i_m, tiles_per_seq) == 0
            u_scr[slot_m, 0:pad, :] = jnp.where(first, 0.0, halo[j_m])
            halo[j_m] = u[tm - pad:, :]

    project(wg_ref, ug_scr, None if sample else halo_g)
    project(wv_ref, uv_scr, None if sample else halo_v)

    def conv_prompt(u_scr, cw_ref, cb_ref, buf_ref):
        u = u_scr[slot_e, pl.ds(pad, tm), :]
        buf_ref[...] = u_scr[slot_e, pl.ds(pad + tm - (FFN_CONV - 1), FFN_CONV - 1), :]
        cw = cw_ref[...]
        out = cb_ref[...]
        for k in range(FFN_CONV - 1):
            out = out + cw[k:k + 1, :] * u_scr[slot_e, pl.ds(pad - (FFN_CONV - 1) + k, tm), :]
        return out + cw[FFN_CONV - 1:FFN_CONV, :] * u

    def conv_sample(u_scr, cw_ref, cb_ref, st_ref, buf_ref):
        u3 = u_scr[slot_e, pl.ds(pad, tm), :].reshape(tm // DEC_SEQ, DEC_SEQ, tf)
        ext = jnp.concatenate([st_ref[...], u3], axis=1)
        buf_ref[...] = u3[:, DEC_SEQ - (FFN_CONV - 1):, :]
        cw = cw_ref[...]
        out = cb_ref[...][None]
        for k in range(FFN_CONV):
            out = out + cw[k:k + 1, :][None] * ext[:, k:k + DEC_SEQ, :]
        return out.reshape(tm, tf)

    if sample:
        gate = conv_sample(ug_scr, cwg_ref, cbg_ref, stg_ref, bufg_ref)
        val = conv_sample(uv_scr, cwv_ref, cbv_ref, stv_ref, bufv_ref)
    else:
        gate = conv_prompt(ug_scr, cwg_ref, cbg_ref, bufg_ref)
        val = conv_prompt(uv_scr, cwv_ref, cbv_ref, bufv_ref)
    act_ref[...] = (jax.nn.silu(gate) * val).astype(BF16)


def _ffn_up(h2, w_up, conv_w, conv_b, state=None):
    m = h2.shape[0]
    sample = state is not None
    tm, tf = (FF_TM_SAMPLE if sample else FF_TM_PROMPT), FF_TF
    n_col = D_FF // tf
    n_row = m // tm
    n_tiles = n_row * n_col

    def mm(t):
        tile = jnp.minimum(t, n_tiles - 1)
        return tile // n_col, tile % n_col

    def ep(t):
        tile = jnp.maximum(t - 1, 0)
        return tile // n_col, tile % n_col

    def col_specs(shape, which, half):
        off = half * n_col
        if len(shape) == 2:
            return pl.BlockSpec(shape, lambda t: (0, which(t)[1] + off))
        return pl.BlockSpec(shape, lambda t: (which(t)[0], 0, which(t)[1] + off))

    in_specs = [pl.BlockSpec((tm, D_MODEL), lambda t: (mm(t)[0], 0), pipeline_mode=pl.Buffered(1)),
                col_specs((D_MODEL, tf), mm, 0), col_specs((D_MODEL, tf), mm, 1),
                col_specs((FFN_CONV, tf), ep, 0), col_specs((FFN_CONV, tf), ep, 1),
                col_specs((1, tf), ep, 0), col_specs((1, tf), ep, 1)]
    args = [h2, w_up, w_up, conv_w, conv_w, conv_b, conv_b]
    scratch = [pltpu.VMEM((2, tm + SUBLANES, tf), F32), pltpu.VMEM((2, tm + SUBLANES, tf), F32)]
    if sample:
        seqs = tm // DEC_SEQ
        n_buf = m // DEC_SEQ
        in_specs += [col_specs((seqs, FFN_CONV - 1, tf), ep, 0), col_specs((seqs, FFN_CONV - 1, tf), ep, 1)]
        args += [state, state]
        buf_spec = lambda: col_specs((seqs, FFN_CONV - 1, tf), ep, 0)
        tiles_per_seq = 1
    else:
        tiles_per_seq = SEQ // tm
        n_buf = n_row
        buf_spec = lambda: col_specs((None, FFN_CONV - 1, tf), ep, 0)
        scratch += [pltpu.VMEM((n_col, SUBLANES, tf), F32), pltpu.VMEM((n_col, SUBLANES, tf), F32)]
    buf_shape = jax.ShapeDtypeStruct((n_buf, FFN_CONV - 1, D_FF), F32)
    return pl.pallas_call(
        functools.partial(_ffn_up_kernel, sample=sample, tiles_per_seq=tiles_per_seq, n_col=n_col, n_tiles=n_tiles),
        grid=(n_tiles + 1,),
        in_specs=in_specs,
        out_specs=[pl.BlockSpec((tm, tf), lambda t: ep(t)), buf_spec(), buf_spec()],
        out_shape=[jax.ShapeDtypeStruct((m, D_FF), BF16), buf_shape, buf_shape],
        scratch_shapes=scratch,
        compiler_params=_cparams(1),
        name="ffn_up_sample" if sample else "ffn_up_prompt",
    )(*args)


def _ffn_down_kernel(act_ref, w_ref, x_ref, g_ref, o_ref, *, k_last_valid):
    k = pl.program_id(1)
    last = pl.num_programs(1) - 1

    @pl.when(k == 0)
    def _():
        o_ref[...] = x_ref[...]

    def accumulate(k_valid):
        act = act_ref[:, :k_valid]
        for c in range(D_MODEL // FF_DOWN_CHUNK):
            cols = slice(c * FF_DOWN_CHUNK, (c + 1) * FF_DOWN_CHUNK)
            o_ref[:, cols] += jnp.dot(act, w_ref[:k_valid, cols], preferred_element_type=F32)

    @pl.when(k < last)
    def _():
        accumulate(FF_TK)

    @pl.when(k == last)
    def _():
        accumulate(k_last_valid)
        for r in range(0, o_ref.shape[0], FF_NORM_ROWS):
            o_ref[r:r + FF_NORM_ROWS, :] = _rms(o_ref[r:r + FF_NORM_ROWS, :], g_ref[...])


def _ffn_down(act, w_down, x2, g_final, name):
    m, k_tot = act.shape
    n_k = pl.cdiv(k_tot, FF_TK)
    return pl.pallas_call(
        functools.partial(_ffn_down_kernel, k_last_valid=k_tot - (n_k - 1) * FF_TK),
        grid=(m // TM, n_k),
        in_specs=[
            pl.BlockSpec((TM, FF_TK), lambda i, k: (i, k)),
            pl.BlockSpec((FF_TK, D_MODEL), lambda i, k: (k, 0)),
            pl.BlockSpec((TM, D_MODEL), lambda i, k: (i, 0)),
            pl.BlockSpec((1, D_MODEL), lambda i, k: (0, 0)),
        ],
        out_specs=pl.BlockSpec((TM, D_MODEL), lambda i, k: (i, 0)),
        out_shape=jax.ShapeDtypeStruct((m, D_MODEL), F32),
        compiler_params=_cparams(2),
        name=name,
    )(act, w_down, x2, g_final)


def _swap_pairs(w):
    shp = w.shape
    return w.reshape(shp[:-1] + (shp[-1] // 2, 2))[..., ::-1].reshape(shp)


def _rope_tables(pos):
    half = QK_ROPE // 2
    freqs = ROPE_THETA ** (-jnp.arange(half, dtype=F32) * 2.0 / QK_ROPE)
    ang = pos.astype(F32)[:, None] * freqs
    cos, sin = jnp.cos(ang), jnp.sin(ang)
    cos_t = jnp.repeat(cos, 2, axis=-1)
    sin_t = jnp.stack([-sin, sin], axis=-1).reshape(pos.shape[0], QK_ROPE)
    return cos_t, sin_t


def kernel(x_prompt, x_sample, cache_ckv, cache_kpe, page_table, state_lru_h, state_lru_conv, state_ffn_conv, g_mix, w_in, g_q, w_uq, g_kv, w_uk, w_uv, lru_conv_w, lru_conv_b, lru_w_a, lru_b_a, lru_w_x, lru_b_x, lru_lambda, g_attn_out, g_lru_out, w_o, g_ffn, w_up, ffn_conv_w, ffn_conv_b, w_down, g_final):
    assert w_in.shape[0] == 1, "single-layer trunk"
    n_p, n_s = x_prompt.shape[0], x_sample.shape[0]
    row = lambda v: v.reshape(1, -1)

    wi = w_in[0]
    c0, c1, c2, c3 = Q_LORA, Q_LORA + KV_LORA, Q_LORA + KV_LORA + QK_ROPE, Q_LORA + KV_LORA + QK_ROPE + LRU_WIDTH
    w_kpe = wi[:, c1:c2]
    zpad = jnp.zeros((D_MODEL, KV_LORA - 2 * QK_ROPE), F32)
    w_in_p = jnp.concatenate([wi[:, :c1], w_kpe, _swap_pairs(w_kpe), zpad, wi[:, c2:c3], wi[:, c3:]],
                             axis=1).astype(BF16)
    wq = w_uq[0]
    wq_rope = wq[:, :, QK_NOPE:]
    w_q_p = jnp.concatenate([wq[:, :, :QK_NOPE].reshape(Q_LORA, -1), wq_rope.reshape(Q_LORA, -1),
                             _swap_pairs(wq_rope).reshape(Q_LORA, -1)], axis=1).astype(BF16)
    w_kv = jnp.concatenate([w_uk[0].reshape(KV_LORA, -1), w_uv[0].reshape(KV_LORA, -1)], axis=1).astype(BF16)
    w_ukt = jnp.transpose(w_uk[0], (1, 2, 0)).astype(BF16)
    w_uvh = jnp.transpose(w_uv[0], (1, 0, 2)).astype(BF16)
    wax = jnp.concatenate([lru_w_a[0], lru_w_x[0]], axis=-1).astype(BF16)
    w_o_b = w_o[0].astype(BF16)
    w_up_b = w_up[0].astype(BF16)
    w_down_b = w_down[0].astype(BF16)

    cos_p, sin_p = _rope_tables(jnp.arange(SEQ))
    cos_s, sin_s = _rope_tables(PAST_LEN + jnp.arange(DEC_SEQ))
    cos_s, sin_s = jnp.tile(cos_s, (n_s, 1)), jnp.tile(sin_s, (n_s, 1))

    lru_args = (lru_conv_w[0], row(lru_conv_b[0]), wax, row(lru_b_a[0]), row(lru_b_x[0]), row(lru_lambda[0]))

    def trunk(x, cos_k, sin_k, tag):
        z = _norm_matmul([(x, 0, D_MODEL, row(g_mix[0]))], w_in_p, None, IN_TN, F32, "in_proj_" + tag)
        qn, qp = _q_proj(z, row(g_q[0]), w_q_p, jnp.tile(cos_k, (1, MLA_HEADS)), jnp.tile(sin_k, (1, MLA_HEADS)),
                         "q_proj_" + tag)
        ckv, kpe = _kv_post(z, row(g_kv[0]), cos_k, sin_k, "kv_post_" + tag)
        return z, qn, qp, ckv, kpe

    def tail(x, o_attn, o_lru, tag, state=None):
        x2 = _norm_matmul([(o_attn, 0, ATTN_WIDTH, row(g_attn_out[0])), (o_lru, 0, LRU_WIDTH, row(g_lru_out[0]))],
                          w_o_b, x, IN_TN, F32, "out_proj_" + tag)
        h2 = _norm_cast(x2, row(g_ffn[0]), "ffn_norm_" + tag)
        act, buf_g, buf_v = _ffn_up(h2, w_up_b, ffn_conv_w[0], row(ffn_conv_b[0]), state)
        y = _ffn_down(act, w_down_b, x2, row(g_final), "ffn_down_" + tag)
        if state is None:
            per_seq = SEQ // FF_TM_PROMPT
            buf_g, buf_v = buf_g[per_seq - 1::per_seq], buf_v[per_seq - 1::per_seq]
        new_buf = jnp.concatenate([buf_g, buf_v], axis=-1)
        return y, new_buf

    xp = x_prompt.reshape(n_p * SEQ, D_MODEL)
    z_p, qn_p, qp_p, ckv_p, kpe_p = trunk(xp, cos_p, sin_p, "p")
    kn_p, v_p = _kv_up(ckv_p, w_kv)
    o_attn_p = _prompt_attn(qn_p, qp_p, kn_p, kpe_p, v_p)
    o_lru_p, hl_p, lbuf_p = _lru_prompt(z_p, *lru_args)
    y_p, fbuf_p = tail(xp, o_attn_p, o_lru_p, "p")

    xs = x_sample.reshape(n_s * DEC_SEQ, D_MODEL)
    z_s, qn_s, qp_s, ckv_s, kpe_s = trunk(xs, cos_s, sin_s, "s")
    q_lat = _q_lat(qn_s, w_ukt)
    q_pe = qp_s.reshape(MLA_HEADS, n_s, DEC_SEQ, QK_ROPE).transpose(1, 0, 2, 3).reshape(
        n_s, MLA_HEADS * DEC_SEQ, QK_ROPE)
    o_lat = _paged_attn(page_table, q_lat, q_pe, ckv_s, kpe_s, cache_ckv, jnp.swapaxes(cache_kpe, 2, 3))
    o_attn_s = _o_up(o_lat, w_uvh)
    o_lru_s, hl_s, lbuf_s = _lru_sample(z_s, state_lru_conv[0], state_lru_h[0].reshape(n_s, 1, LRU_WIDTH), *lru_args)
    y_s, fbuf_s = tail(xs, o_attn_s, o_lru_s, "s", state_ffn_conv[0])

    return (
        y_p.reshape(n_p, SEQ, D_MODEL),
        y_s.reshape(n_s, DEC_SEQ, D_MODEL),
        ckv_p.reshape(1, n_p, SEQ, KV_LORA),
        kpe_p.reshape(1, n_p, SEQ, QK_ROPE),
        ckv_s.reshape(1, n_s, DEC_SEQ, KV_LORA),
        kpe_s.reshape(1, n_s, DEC_SEQ, QK_ROPE),
        hl_p.reshape(1, n_p, LRU_WIDTH),
        hl_s.reshape(1, n_s, LRU_WIDTH),
        lbuf_p[None],
        lbuf_s[None],
        fbuf_p[None],
        fbuf_s[None],
    )
```
